```python
import jax, jax.numpy as jnp
from jax import lax
import numpy as np

D_MODEL = 1024
BATCH = 8
SEQ = 4096
DEPTH = 2

F32 = jnp.float32
N_META = 16
BLOCK = 128
FRONT_PAD = (-N_META) % BLOCK
CONV_CH = D_MODEL // 2
CONV_WIDTH = 31
RWKV_HEAD = 64
RWKV_CH = D_MODEL - CONV_CH
RWKV_HEADS = RWKV_CH // RWKV_HEAD
DECAY_RANK = 64
AAA_RANK = 64
GATE_RANK = 128
RWKV_COLS = 3 * RWKV_CH + DECAY_RANK + AAA_RANK + GATE_RANK
IN0_COLS = 2 * CONV_CH + RWKV_COLS
HEAD_DIM = 64
N_Q_HEADS = D_MODEL // HEAD_DIM
N_KV_HEADS = max(1, N_Q_HEADS // 8)
GROUP = N_Q_HEADS // N_KV_HEADS
WINDOW = 128
ROPE_THETA = 10000.0
QKV_COLS = (N_Q_HEADS + 2 * N_KV_HEADS) * HEAD_DIM
D_FF_DENSE = ((8 * D_MODEL // 3 + 255) // 256) * 256
D_FF_EXPERT = 7 * D_MODEL // 2
N_EXPERTS = 8
TOP_K = 2
N_EVEN = (DEPTH + 1) // 2
N_ODD = DEPTH // 2
ALPHA = (2 * DEPTH) ** 0.25
BETA = (8 * DEPTH) ** -0.25
LN_EPS = 1e-5
LNX_EPS = 64e-5

kernel_name = "hybrid_conv_rwkv7_swa_sink_moe_deepnorm"


def layer_norm(x, g, b, eps=LN_EPS):
    xf = x.astype(F32)
    mu = jnp.mean(xf, -1, keepdims=True)
    var = jnp.mean(jnp.square(xf - mu), -1, keepdims=True)
    return ((xf - mu) * lax.rsqrt(var + eps) * g.astype(F32) + b.astype(F32)).astype(x.dtype)


def token_shift(z):
    return jnp.pad(z[:, :-1], ((0, 0), (1, 0), (0, 0)))


def conformer_conv(u, conv_w, conv_b, norm_g, norm_b):
    val, gate = jnp.split(u, 2, axis=-1)
    h = val * jax.nn.sigmoid(gate)
    h = lax.conv_general_dilated(h, conv_w[:, None, :], window_strides=(1,),
                                 padding=[(CONV_WIDTH - 1, 0)],
                                 dimension_numbers=('NWC', 'WIO', 'NWC'),
                                 feature_group_count=CONV_CH) + conv_b
    return jax.nn.silu(layer_norm(h, norm_g, norm_b))


def rwkv7_time_mix(z, shift_mu, w0, w2, a0, a2, g2, k_k, k_a, r_k, lnx_g, lnx_b):
    bsz, seqlen = z.shape[:2]
    z = z + (token_shift(z) - z) * shift_mu
    cuts = [int(c) for c in np.cumsum([RWKV_CH, RWKV_CH, RWKV_CH, DECAY_RANK, AAA_RANK])]
    r, k, v, w_lo, a_lo, g_lo = jnp.split(z, cuts, axis=-1)
    w_log = -jax.nn.softplus(-(w0 + jnp.tanh(w_lo) @ w2)) - 0.5
    decay = jnp.exp(-jnp.exp(w_log.astype(F32)))
    a = jax.nn.sigmoid(a0 + a_lo @ a2)
    g = jax.nn.sigmoid(g_lo) @ g2

    def heads(t):
        return t.reshape(bsz, seqlen, RWKV_HEADS, RWKV_HEAD).astype(F32)

    kk = heads(k * k_k)
    kk = kk / jnp.maximum(jnp.sqrt(jnp.sum(kk * kk, -1, keepdims=True)), 1e-12)
    k = k * (1.0 + (a - 1.0) * k_a)
    r_h, k_h, v_h, a_h, w_h = heads(r), heads(k), heads(v), heads(a), heads(decay)
    kka = kk * a_h

    def step(S, inp):
        r_t, w_t, k_t, v_t, kk_t, kka_t = inp
        sa = jnp.einsum('bhij,bhj->bhi', S, -kk_t)
        S = S * w_t[:, :, None, :] + sa[..., None] * kka_t[:, :, None, :] + v_t[..., None] * k_t[:, :, None, :]
        return S, jnp.einsum('bhij,bhj->bhi', S, r_t)

    xs = tuple(jnp.moveaxis(t, 1, 0) for t in (r_h, w_h, k_h, v_h, kk, kka))
    S0 = jnp.zeros((bsz, RWKV_HEADS, RWKV_HEAD, RWKV_HEAD), F32)
    _, y = lax.scan(step, S0, xs)
    y = jnp.moveaxis(y, 0, 1)
    mu = jnp.mean(y, -1, keepdims=True)
    var = jnp.mean(jnp.square(y - mu), -1, keepdims=True)
    y = ((y - mu) * lax.rsqrt(var + LNX_EPS)).reshape(bsz, seqlen, RWKV_CH) * lnx_g.astype(F32) + lnx_b.astype(F32)
    bonus = jnp.sum(r_h * k_h * r_k.astype(F32), -1, keepdims=True) * v_h
    y = (y + bonus.reshape(bsz, seqlen, RWKV_CH)) * g.astype(F32)
    return y.astype(z.dtype)


def rope(t, pos):
    half = HEAD_DIM // 2
    inv = ROPE_THETA ** (-jnp.arange(half, dtype=F32) / half)
    ang = pos.astype(F32)[:, None] * inv[None, :]
    cos = jnp.cos(ang)[None, :, None, :]
    sin = jnp.sin(ang)[None, :, None, :]
    t1 = t[..., :half].astype(F32)
    t2 = t[..., half:].astype(F32)
    return jnp.concatenate([t1 * cos - t2 * sin, t2 * cos + t1 * sin], -1).astype(t.dtype)


def sliding_window_sink_attention(h, w_qkv, b_qkv, sinks):
    bsz, seqlen, _ = h.shape
    qkv = h @ w_qkv + b_qkv
    q, k, v = jnp.split(qkv, [N_Q_HEADS * HEAD_DIM, (N_Q_HEADS + N_KV_HEADS) * HEAD_DIM], axis=-1)
    q = q.reshape(bsz, seqlen, N_Q_HEADS, HEAD_DIM)
    k = k.reshape(bsz, seqlen, N_KV_HEADS, HEAD_DIM)
    v = v.reshape(bsz, seqlen, N_KV_HEADS, HEAD_DIM)
    pos = jnp.arange(seqlen)
    q, k = rope(q, pos), rope(k, pos)
    padw = ((0, 0), (FRONT_PAD, 0), (0, 0), (0, 0))
    q, k, v = jnp.pad(q, padw), jnp.pad(k, padw), jnp.pad(v, padw)
    lp = seqlen + FRONT_PAD
    nb = lp // BLOCK
    qb = q.reshape(bsz, nb, BLOCK, N_KV_HEADS, GROUP, HEAD_DIM)

    def band(t):
        tb = t.reshape(bsz, nb, BLOCK, N_KV_HEADS, HEAD_DIM)
        prev = jnp.pad(tb[:, :-1], ((0, 0), (1, 0), (0, 0), (0, 0), (0, 0)))
        return jnp.concatenate([prev, tb], axis=2)

    kw, vw = band(k), band(v)
    s = jnp.einsum('bnqkgd,bnskd->bnkgqs', qb, kw, preferred_element_type=F32) * (HEAD_DIM ** -0.5)
    blk = jnp.arange(nb)[:, None, None]
    qi = blk * BLOCK + jnp.arange(BLOCK)[None, :, None]
    kj = (blk - 1) * BLOCK + jnp.arange(2 * BLOCK)[None, None, :]
    valid = (kj <= qi) & (kj > qi - WINDOW) & (kj >= FRONT_PAD)
    s = jnp.where(valid[None, :, None, None], s, -jnp.inf)
    sink = sinks.astype(F32).reshape(N_KV_HEADS, GROUP)[None, None, :, :, None, None]
    m = jnp.maximum(jnp.max(s, -1, keepdims=True), sink)
    p = jnp.exp(s - m)
    p = p / (jnp.sum(p, -1, keepdims=True) + jnp.exp(sink - m))
    o = jnp.einsum('bnkgqs,bnskd->bnqkgd', p.astype(vw.dtype), vw)
    return o.reshape(bsz, lp, N_Q_HEADS * HEAD_DIM)[:, FRONT_PAD:]


def swiglu(h, w_gate, w_up, w_down):
    return (jax.nn.silu(h @ w_gate) * (h @ w_up)) @ w_down


def moe_swiglu(h, router, e_gate, e_up, e_down):
    bsz, seqlen, d = h.shape
    xf = h.reshape(-1, d)
    logits = jnp.dot(xf, router, preferred_element_type=F32)
    top_v, top_i = lax.top_k(logits, TOP_K)
    top_w = jax.nn.softmax(top_v, axis=-1)
    gates = jnp.sum(jax.nn.one_hot(top_i, N_EXPERTS, dtype=F32) * top_w[..., None], axis=1)
    y = jnp.zeros(xf.shape, F32)
    for e in range(N_EXPERTS):
        y = y + gates[:, e:e + 1] * swiglu(xf, e_gate[e], e_up[e], e_down[e]).astype(F32)
    return y.reshape(bsz, seqlen, d).astype(h.dtype)


def setup_inputs(seed: int = 0) -> dict:
    key = jax.random.key(seed)
    keys = iter(jax.random.split(key, 64))

    def nrm(shape, scale):
        return jax.random.normal(next(keys), shape, F32) * scale

    def gain(shape):
        return 1.0 + nrm(shape, 0.02)

    E, O = N_EVEN, N_ODD
    return {
        "x": nrm((BATCH, SEQ, D_MODEL), 1.0),
        "meta_tokens": nrm((N_META, D_MODEL), 1.0),
        "ev_w_in": nrm((E, D_MODEL, IN0_COLS), D_MODEL ** -0.5),
        "ev_conv_w": nrm((E, CONV_WIDTH, CONV_CH), CONV_WIDTH ** -0.5),
        "ev_conv_b": nrm((E, CONV_CH), 0.02),
        "ev_convnorm_g": gain((E, CONV_CH)),
        "ev_convnorm_b": nrm((E, CONV_CH), 0.02),
        "ev_shift_mu": jax.random.uniform(next(keys), (E, RWKV_COLS), F32),
        "ev_w0": jax.random.uniform(next(keys), (E, RWKV_CH), F32, -6.0, -1.0),
        "ev_w2": nrm((E, DECAY_RANK, RWKV_CH), 0.5 * DECAY_RANK ** -0.5),
        "ev_a0": nrm((E, RWKV_CH), 0.1),
        "ev_a2": nrm((E, AAA_RANK, RWKV_CH), 0.5 * AAA_RANK ** -0.5),
        "ev_g2": nrm((E, GATE_RANK, RWKV_CH), GATE_RANK ** -0.5),
        "ev_k_k": 0.85 + nrm((E, RWKV_CH), 0.02),
        "ev_k_a": gain((E, RWKV_CH)),
        "ev_r_k": nrm((E, RWKV_HEADS, RWKV_HEAD), 0.1),
        "ev_lnx_g": gain((E, RWKV_CH)),
        "ev_lnx_b": nrm((E, RWKV_CH), 0.02),
        "ev_w_out": nrm((E, D_MODEL, D_MODEL), BETA * D_MODEL ** -0.5),
        "ev_ln1_g": gain((E, D_MODEL)),
        "ev_ln1_b": nrm((E, D_MODEL), 0.02),
        "ev_ffn_gate": nrm((E, D_MODEL, D_FF_DENSE), D_MODEL ** -0.5),
        "ev_ffn_up": nrm((E, D_MODEL, D_FF_DENSE), D_MODEL ** -0.5),
        "ev_ffn_down": nrm((E, D_FF_DENSE, D_MODEL), BETA * D_FF_DENSE ** -0.5),
        "ev_ln2_g": gain((E, D_MODEL)),
        "ev_ln2_b": nrm((E, D_MODEL), 0.02),
        "od_w_qkv": nrm((O, D_MODEL, QKV_COLS), D_MODEL ** -0.5),
        "od_b_qkv": nrm((O, QKV_COLS), 0.02),
        "od_sinks": nrm((O, N_Q_HEADS), 1.0),
        "od_w_o": nrm((O, N_Q_HEADS * HEAD_DIM, D_MODEL), BETA * (N_Q_HEADS * HEAD_DIM) ** -0.5),
        "od_b_o": nrm((O, D_MODEL), 0.02),
        "od_ln1_g": gain((O, D_MODEL)),
        "od_ln1_b": nrm((O, D_MODEL), 0.02),
        "od_router": nrm((O, D_MODEL, N_EXPERTS), D_MODEL ** -0.5),
        "od_exp_gate": nrm((O, N_EXPERTS, D_MODEL, D_FF_EXPERT), D_MODEL ** -0.5),
        "od_exp_up": nrm((O, N_EXPERTS, D_MODEL, D_FF_EXPERT), D_MODEL ** -0.5),
        "od_exp_down": nrm((O, N_EXPERTS, D_FF_EXPERT, D_MODEL), BETA * D_FF_EXPERT ** -0.5),
        "od_ln2_g": gain((O, D_MODEL)),
        "od_ln2_b": nrm((O, D_MODEL), 0.02),
    }


def reference(x, meta_tokens, ev_w_in, ev_conv_w, ev_conv_b, ev_convnorm_g, ev_convnorm_b,
              ev_shift_mu, ev_w0, ev_w2, ev_a0, ev_a2, ev_g2, ev_k_k, ev_k_a, ev_r_k,
              ev_lnx_g, ev_lnx_b, ev_w_out, ev_ln1_g, ev_ln1_b, ev_ffn_gate, ev_ffn_up,
              ev_ffn_down, ev_ln2_g, ev_ln2_b, od_w_qkv, od_b_qkv, od_sinks, od_w_o, od_b_o,
              od_ln1_g, od_ln1_b, od_router, od_exp_gate, od_exp_up, od_exp_down,
              od_ln2_g, od_ln2_b):
    bsz = x.shape[0]
    meta = jnp.broadcast_to(meta_tokens[None].astype(x.dtype), (bsz, N_META, D_MODEL))
    h = jnp.concatenate([meta, x], axis=1)
    for layer in range(DEPTH):
        i = layer // 2
        if layer % 2 == 0:
            u = h @ ev_w_in[i]
            a_out = conformer_conv(u[..., :2 * CONV_CH], ev_conv_w[i], ev_conv_b[i],
                                   ev_convnorm_g[i], ev_convnorm_b[i])
            b_out = rwkv7_time_mix(u[..., 2 * CONV_CH:], ev_shift_mu[i], ev_w0[i], ev_w2[i],
                                   ev_a0[i], ev_a2[i], ev_g2[i], ev_k_k[i], ev_k_a[i], ev_r_k[i],
                                   ev_lnx_g[i], ev_lnx_b[i])
            mix = jnp.concatenate([a_out, b_out], axis=-1) @ ev_w_out[i]
            h = layer_norm(ALPHA * h + mix, ev_ln1_g[i], ev_ln1_b[i])
            f = swiglu(h, ev_ffn_gate[i], ev_ffn_up[i], ev_ffn_down[i])
            h = layer_norm(ALPHA * h + f, ev_ln2_g[i], ev_ln2_b[i])
        else:
            att = sliding_window_sink_attention(h, od_w_qkv[i], od_b_qkv[i], od_sinks[i])
            mix = att @ od_w_o[i] + od_b_o[i]
            h = layer_norm(ALPHA * h + mix, od_ln1_g[i], od_ln1_b[i])
            f = moe_swiglu(h, od_router[i], od_exp_gate[i], od_exp_up[i], od_exp_down[i])
            h = layer_norm(ALPHA * h + f, od_ln2_g[i], od_ln2_b[i])
    return h[:, N_META:]
```

```python
import functools
import math

import jax
import jax.numpy as jnp
import numpy as np
from jax import lax
from jax.experimental import pallas as pl
from jax.experimental.pallas import tpu as pltpu

F32 = jnp.float32
BF16 = jnp.bfloat16

D_MODEL = 1024
N_META = 16
ATT_BLOCK = 128
FRONT_PAD = (-N_META) % ATT_BLOCK
CONV_CH = 512
CONV_WIDTH = 31
RWKV_CH = 512
RWKV_HEAD = 64
RWKV_COLS = 3 * RWKV_CH + 64 + 64 + 128
HEAD_DIM = 64
N_Q_HEADS = 16
N_KV_HEADS = 2
GROUP = N_Q_HEADS // N_KV_HEADS
ROPE_THETA = 10000.0
N_EXPERTS = 8
DEPTH = 2
ALPHA = (2 * DEPTH) ** 0.25
LN_EPS = 1e-5
LNX_EPS = 64e-5

CHUNK = 64
HEADS_PER_GROUP = 4
GROUP_W = HEADS_PER_GROUP * RWKV_HEAD
VMEM_LIMIT = 56 * 1024 * 1024

NN = (((1,), (0,)), ((), ()))
NT = (((1,), (1,)), ((), ()))
TN = (((0,), (0,)), ((), ()))


def _params(*sem):
    return pltpu.CompilerParams(dimension_semantics=sem, vmem_limit_bytes=VMEM_LIMIT)


def _sigmoid(x):
    return 1.0 / (1.0 + jnp.exp(-x))


def _split2(x):
    hi = x.astype(BF16)
    lo = (x - hi.astype(F32)).astype(BF16)
    return hi, lo


def _dot(a, b, dims=NN, passes=1):
    if passes == 1:
        return lax.dot_general(a.astype(BF16), b.astype(BF16), dims, preferred_element_type=F32)
    ah, al = _split2(a)
    bh, bl = _split2(b)
    out = lax.dot_general(ah, bh, dims, preferred_element_type=F32)
    out = out + lax.dot_general(ah, bl, dims, preferred_element_type=F32)
    return out + lax.dot_general(al, bh, dims, preferred_element_type=F32)


def _dot_exact_rhs(a, b_bf16, dims=NN, parts=2):
    out = None
    rem = a
    for _ in range(parts):
        piece = rem.astype(BF16)
        term = lax.dot_general(piece, b_bf16, dims, preferred_element_type=F32)
        out = term if out is None else out + term
        rem = rem - piece.astype(F32)
    return out


def _layer_norm(x, g, b, eps):
    mu = jnp.mean(x, axis=-1, keepdims=True)
    xc = x - mu
    var = jnp.mean(xc * xc, axis=-1, keepdims=True)
    return xc * lax.rsqrt(var + eps) * g + b


def _mm_kernel(x_ref, w_ref, o_ref):
    o_ref[...] = jnp.dot(x_ref[...].astype(BF16), w_ref[...],
                         preferred_element_type=F32).astype(o_ref.dtype)


def _matmul(x, w, tm, tn, out_dtype):
    t, k = x.shape
    n = w.shape[1]
    return pl.pallas_call(
        _mm_kernel,
        grid=(t // tm, n // tn),
        in_specs=[pl.BlockSpec((tm, k), lambda i, j: (i, 0)),
                  pl.BlockSpec((k, tn), lambda i, j: (0, j))],
        out_specs=pl.BlockSpec((tm, tn), lambda i, j: (i, j)),
        out_shape=jax.ShapeDtypeStruct((t, n), out_dtype),
        compiler_params=_params("parallel", "parallel"),
        name="matmul",
    )(x, w)


CONV_ROWS = 256
CONV_SUB = 32
CONV_HALO = 32


def _conv_kernel(u_ref, cw_ref, cb_ref, g_ref, b_ref, o_ref, buf_ref):
    @pl.when(pl.program_id(0) == 0)
    def _():
        buf_ref[0:CONV_HALO, :] = jnp.zeros((CONV_HALO, CONV_CH), F32)

    u = u_ref[...]
    buf_ref[CONV_HALO:CONV_HALO + CONV_ROWS, :] = u[:, :CONV_CH] * _sigmoid(u[:, CONV_CH:])
    cw = cw_ref[...]
    base = CONV_HALO - (CONV_WIDTH - 1)
    for s in range(CONV_ROWS // CONV_SUB):
        r0 = s * CONV_SUB
        acc = jnp.broadcast_to(cb_ref[...], (CONV_SUB, CONV_CH))
        for j in range(CONV_WIDTH):
            acc = acc + cw[j:j + 1, :] * buf_ref[base + r0 + j:base + r0 + j + CONV_SUB, :]
        y = _layer_norm(acc, g_ref[...], b_ref[...], LN_EPS)
        o_ref[r0:r0 + CONV_SUB, :] = (y * _sigmoid(y)).astype(o_ref.dtype)
    buf_ref[0:CONV_HALO, :] = buf_ref[CONV_ROWS:CONV_ROWS + CONV_HALO, :]


def _conv_module(u_conv, conv_w, conv_b, norm_g, norm_b):
    t = u_conv.shape[0]
    row = lambda v: v.reshape(1, -1)
    full = lambda shape: pl.BlockSpec(shape, lambda i: (0, 0))
    return pl.pallas_call(
        _conv_kernel,
        grid=(t // CONV_ROWS,),
        in_specs=[pl.BlockSpec((CONV_ROWS, 2 * CONV_CH), lambda i: (i, 0)),
                  full((CONV_WIDTH, CONV_CH)), full((1, CONV_CH)), full((1, CONV_CH)),
                  full((1, CONV_CH))],
        out_specs=pl.BlockSpec((CONV_ROWS, CONV_CH), lambda i: (i, 0)),
        out_shape=jax.ShapeDtypeStruct((t, CONV_CH), BF16),
        scratch_shapes=[pltpu.VMEM((CONV_ROWS + CONV_HALO, CONV_CH), F32)],
        compiler_params=_params("arbitrary"),
        name="conv_module",
    )(u_conv, conv_w, row(conv_b), row(norm_g), row(norm_b))


def _rwkv_masks():
    n = GROUP_W
    r = np.arange(n)[:, None]
    c = np.arange(n)[None, :]
    same = lambda w: (r // w) == (c // w)
    strict = (r > c) & same(CHUNK)
    incl = (r >= c) & same(CHUNK)
    m16 = (r > c) & same(16)
    m32 = (r > c) & same(32) & ~same(16)
    m64 = (r > c) & same(64) & ~same(32)
    stack = (c // RWKV_HEAD) == (r // CHUNK)
    masks = np.stack([strict, incl, m16, m32, m64, stack]).astype(np.float32)
    tri = (np.arange(CHUNK)[:, None] >= np.arange(CHUNK)[None, :]).astype(np.float32)
    hr = np.arange(RWKV_CH)
    head_ones = ((hr[:, None] // RWKV_HEAD) == (hr[None, :] // RWKV_HEAD)).astype(np.float32)
    return masks, tri, head_ones


def _unit_lower_inverse_minus_identity(lab, m16, m32, m64, passes):
    mm = functools.partial(_dot, passes=passes)
    d = lab * m16
    d2 = mm(d, d)
    d4 = mm(d2, d2)
    d8 = mm(d4, d4)
    n = d2 - d - mm(d, d2)
    n = n + d4 + mm(n, d4)
    n = n + d8 + mm(n, d8)
    for off in (lab * m32, lab * m64):
        w = off + mm(n, off)
        w = w + mm(w, n)
        n = n - w
    return n


def _rwkv_kernel(u_ref, mu_ref, w0_ref, w2_ref, a0_ref, a2_ref, g2_ref, kk_ref, ka_ref, rk_ref,
                 lg_ref, lb_ref, masks_ref, tri_ref, ones_ref, o_ref, s_ref, carry_ref):
    c = CHUNK

    @pl.when(pl.program_id(1) == 0)
    def _():
        s_ref[...] = jnp.zeros_like(s_ref)
        carry_ref[...] = jnp.zeros_like(carry_ref)

    z = u_ref[...]
    prev = pltpu.roll(z, 1, 0)
    row_id = lax.broadcasted_iota(jnp.int32, z.shape, 0)
    prev = jnp.where(row_id == 0, jnp.broadcast_to(carry_ref[7:8, :], z.shape), prev)
    carry_ref[...] = z[c - 8:c, :]
    z = z + (prev - z) * mu_ref[...]

    r = z[:, 0:RWKV_CH]
    k = z[:, RWKV_CH:2 * RWKV_CH]
    v = z[:, 2 * RWKV_CH:3 * RWKV_CH]
    wa_lo = z[:, 3 * RWKV_CH:3 * RWKV_CH + 128]
    g_lo = z[:, 3 * RWKV_CH + 128:3 * RWKV_CH + 256]

    wpre = w0_ref[...] + _dot(jnp.tanh(wa_lo), w2_ref[...], passes=3)
    ew = _sigmoid(wpre) * math.exp(-0.5)
    a = _sigmoid(a0_ref[...] + _dot(wa_lo, a2_ref[...], passes=3))
    gate = _dot(_sigmoid(g_lo), g2_ref[...], passes=3)

    ones = ones_ref[...]
    kk = k * kk_ref[...]
    ss = _dot_exact_rhs(kk * kk, ones)
    kk = kk * lax.rsqrt(jnp.maximum(ss, 1e-24))
    k2 = k * (1.0 + (a - 1.0) * ka_ref[...])
    kka = kk * a

    e0 = ew.astype(BF16)
    e1f = ew - e0.astype(F32)
    e1 = e1f.astype(BF16)
    e2 = (e1f - e1.astype(F32)).astype(BF16)
    tri = tri_ref[...]
    cum = (jnp.dot(tri, e0, preferred_element_type=F32) + jnp.dot(tri, e1, preferred_element_type=F32)
           + jnp.dot(tri, e2, preferred_element_type=F32))
    gam = jnp.exp(-cum)
    igam = jnp.exp(cum)
    r_t = r * gam
    ka_t = kk * jnp.exp(ew - cum)
    b_t = kka * igam
    k_t = k2 * igam
    gam_end = gam[c - 1:c, :]

    strict = masks_ref[0]
    incl = masks_ref[1]
    m16 = masks_ref[2]
    m32 = masks_ref[3]
    m64 = masks_ref[4]
    stack = masks_ref[5]
    p1 = 1

    def stacked(x):
        return jnp.concatenate([x] * HEADS_PER_GROUP, axis=0) * stack

    ys = []
    for g in range(RWKV_CH // GROUP_W):
        lanes = slice(g * GROUP_W, (g + 1) * GROUP_W)
        xa, xr, xv = stacked(ka_t[:, lanes]), stacked(r_t[:, lanes]), stacked(v[:, lanes])
        yb, yk = stacked(b_t[:, lanes]), stacked(k_t[:, lanes])
        lab = _dot(xa, yb, NT, p1) * strict
        lak = _dot(xa, yk, NT, p1) * strict
        arb = _dot(xr, yb, NT, p1) * incl
        ark = _dot(xr, yk, NT, p1) * incl
        n_inv = _unit_lower_inverse_minus_identity(lab, m16, m32, m64, p1)
        lv = _dot(lak, xv, NN, p1)
        rhs = jnp.concatenate([xa, lv], axis=1)
        gz = rhs + _dot(n_inv, rhs, NN, p1)
        gmat, zmat = gz[:, :GROUP_W], gz[:, GROUP_W:]
        corr = _dot(arb, gz, NN, p1)
        qh = xr - corr[:, :GROUP_W]
        yloc = _dot(ark, xv, NN, p1) - corr[:, GROUP_W:]
        s0 = s_ref[g]
        y_st = _dot(qh, s0, NT, p1) + yloc
        t_hat = _dot(gmat, yb, TN, p1)
        u_hat = _dot(xv, yk, TN, p1) - _dot(zmat, yb, TN, p1)
        s_new = s0 - _dot(s0, t_hat, NN, p1) + u_hat
        s_ref[g] = s_new * gam_end[:, lanes]
        y = y_st[0:c]
        for h in range(1, HEADS_PER_GROUP):
            y = y + y_st[h * c:(h + 1) * c]
        ys.append(y)
    y = jnp.concatenate(ys, axis=1)

    inv_n = 1.0 / RWKV_HEAD
    mean = _dot_exact_rhs(y, ones) * inv_n
    yc = y - mean
    var = _dot_exact_rhs(yc * yc, ones) * inv_n
    yn = yc * lax.rsqrt(var + LNX_EPS) * lg_ref[...] + lb_ref[...]
    bonus = _dot_exact_rhs(r * k2 * rk_ref[...], ones) * v
    o_ref[...] = ((yn + bonus) * gate).astype(o_ref.dtype)


def _rwkv_time_mix(u_rwkv, n_batch, shift_mu, w0, w2, a0, a2, g2, k_k, k_a, r_k, lnx_g, lnx_b):
    t = u_rwkv.shape[0]
    n_chunks = t // n_batch // CHUNK
    masks, tri, head_ones = _rwkv_masks()
    row = lambda v: v.reshape(1, -1).astype(F32)
    zeros64 = jnp.zeros((64, RWKV_CH), F32)
    w2p = jnp.concatenate([w2, zeros64], axis=0)
    a2p = jnp.concatenate([zeros64, a2], axis=0)
    full2 = lambda shape: pl.BlockSpec(shape, lambda b, c: (0, 0))
    vec = full2((1, RWKV_CH))
    return pl.pallas_call(
        _rwkv_kernel,
        grid=(n_batch, n_chunks),
        in_specs=[pl.BlockSpec((CHUNK, RWKV_COLS), lambda b, c: (b * n_chunks + c, 0)),
                  full2((1, RWKV_COLS)), vec, full2((128, RWKV_CH)), vec, full2((128, RWKV_CH)),
                  full2((128, RWKV_CH)), vec, vec, vec, vec, vec,
                  pl.BlockSpec(masks.shape, lambda b, c: (0, 0, 0)),
                  full2((CHUNK, CHUNK)), full2((RWKV_CH, RWKV_CH))],
        out_specs=pl.BlockSpec((CHUNK, RWKV_CH), lambda b, c: (b * n_chunks + c, 0)),
        out_shape=jax.ShapeDtypeStruct((t, RWKV_CH), BF16),
        scratch_shapes=[pltpu.VMEM((RWKV_CH // GROUP_W, GROUP_W, GROUP_W), F32),
                        pltpu.VMEM((8, RWKV_COLS), F32)],
        compiler_params=_params("arbitrary", "arbitrary"),
        name="rwkv7_time_mix",
    )(u_rwkv, row(shift_mu), row(w0), w2p, row(a0), a2p, g2, row(k_k), row(k_a), row(r_k),
      row(lnx_g), row(lnx_b), jnp.asarray(masks), jnp.asarray(tri, BF16),
      jnp.asarray(head_ones, BF16))


def _proj_ln_kernel(n_in, has_bias, *refs):
    h_ref = refs[0]
    xs = refs[1:1 + n_in]
    ws = refs[1 + n_in:1 + 2 * n_in]
    rest = refs[1 + 2 * n_in:]
    if has_bias:
        bias_ref, g_ref, b_ref, o_ref = rest
    else:
        g_ref, b_ref, o_ref = rest
    acc = ALPHA * h_ref[...]
    for x_ref, w_ref in zip(xs, ws):
        acc = acc + jnp.dot(x_ref[...].astype(BF16), w_ref[...], preferred_element_type=F32)
    if has_bias:
        acc = acc + bias_ref[...]
    o_ref[...] = _layer_norm(acc, g_ref[...], b_ref[...], LN_EPS)


def _proj_ln(h, xs, ws, bias, g, b, tm):
    t, d = h.shape
    row = lambda v: v.reshape(1, -1)
    full = lambda shape: pl.BlockSpec(shape, lambda i: (0, 0))
    in_specs = [pl.BlockSpec((tm, d), lambda i: (i, 0))]
    in_specs += [pl.BlockSpec((tm, x.shape[1]), lambda i: (i, 0)) for x in xs]
    in_specs += [full(w.shape) for w in ws]
    args = [h, *xs, *ws]
    if bias is not None:
        in_specs.append(full((1, d)))
        args.append(row(bias))
    in_specs += [full((1, d)), full((1, d))]
    args += [row(g), row(b)]
    return pl.pallas_call(
        functools.partial(_proj_ln_kernel, len(xs), bias is not None),
        grid=(t // tm,),
        in_specs=in_specs,
        out_specs=pl.BlockSpec((tm, d), lambda i: (i, 0)),
        out_shape=jax.ShapeDtypeStruct((t, d), F32),
        compiler_params=_params("parallel"),
        name="proj_residual_ln",
    )(*args)


def _swiglu_step(x_bf16, wg_ref, wu_ref, wd_ref):
    gate = jnp.dot(x_bf16, wg_ref[...], preferred_element_type=F32)
    up = jnp.dot(x_bf16, wu_ref[...], preferred_element_type=F32)
    act = (gate * _sigmoid(gate) * up).astype(BF16)
    return jnp.dot(act, wd_ref[...], preferred_element_type=F32)


def _ffn_ln_kernel(h_ref, wg_ref, wu_ref, wd_ref, g_ref, b_ref, o_ref, acc_ref):
    j = pl.program_id(1)
    part = _swiglu_step(h_ref[...].astype(BF16), wg_ref, wu_ref, wd_ref)

    @pl.when(j == 0)
    def _():
        acc_ref[...] = ALPHA * h_ref[...] + part

    @pl.when(j > 0)
    def _():
        acc_ref[...] += part

    @pl.when(j == pl.num_programs(1) - 1)
    def _():
        o_ref[...] = _layer_norm(acc_ref[...], g_ref[...], b_ref[...], LN_EPS)


def _ffn_ln(h, wg, wu, wd, g, b, tm, tf):
    t, d = h.shape
    dff = wg.shape[1]
    row = lambda v: v.reshape(1, -1)
    return pl.pallas_call(
        _ffn_ln_kernel,
        grid=(t // tm, dff // tf),
        in_specs=[pl.BlockSpec((tm, d), lambda i, j: (i, 0)),
                  pl.BlockSpec((d, tf), lambda i, j: (0, j)),
                  pl.BlockSpec((d, tf), lambda i, j: (0, j)),
                  pl.BlockSpec((tf, d), lambda i, j: (j, 0)),
                  pl.BlockSpec((1, d), lambda i, j: (0, 0)),
                  pl.BlockSpec((1, d), lambda i, j: (0, 0))],
        out_specs=pl.BlockSpec((tm, d), lambda i, j: (i, 0)),
        out_shape=jax.ShapeDtypeStruct((t, d), F32),
        scratch_shapes=[pltpu.VMEM((tm, d), F32)],
        compiler_params=_params("parallel", "arbitrary"),
        name="swiglu_residual_ln",
    )(h, wg, wu, wd, row(g), row(b))


def _qkv_rope_kernel(h_ref, w_ref, bias_ref, cos_ref, sin_ref, o_ref):
    acc = jnp.dot(h_ref[...].astype(BF16), w_ref[...], preferred_element_type=F32) + bias_ref[...]
    n_rot = (N_Q_HEADS + 2 * N_KV_HEADS) * HEAD_DIM
    cos = cos_ref[...]
    sin = sin_ref[...]
    lane = lax.broadcasted_iota(jnp.int32, cos.shape, 1)
    first_half = (lane % HEAD_DIM) < (HEAD_DIM // 2)
    for c0 in range(0, n_rot, 128):
        x = acc[:, c0:c0 + 128]
        swapped = jnp.where(first_half, pltpu.roll(x, 128 - HEAD_DIM // 2, 1),
                            pltpu.roll(x, HEAD_DIM // 2, 1))
        o_ref[:, c0:c0 + 128] = (x * cos + swapped * sin).astype(o_ref.dtype)
    o_ref[:, n_rot:] = acc[:, n_rot:].astype(o_ref.dtype)


def _rope_tables(lp):
    half = HEAD_DIM // 2
    inv = ROPE_THETA ** (-jnp.arange(half, dtype=F32) / half)
    pos = (jnp.arange(lp) - FRONT_PAD).astype(F32)
    ang = pos[:, None] * inv[None, :]
    cos = jnp.cos(ang)
    sin = jnp.sin(ang)
    cos = jnp.concatenate([cos, cos] * (128 // HEAD_DIM), axis=1)
    sin = jnp.concatenate([-sin, sin] * (128 // HEAD_DIM), axis=1)
    return cos, sin


def _qkv_rope(h, w, bias, lp, tm):
    t, d = h.shape
    n = w.shape[1]
    cos, sin = _rope_tables(lp)
    per_seq = lp // tm
    return pl.pallas_call(
        _qkv_rope_kernel,
        grid=(t // tm,),
        in_specs=[pl.BlockSpec((tm, d), lambda i: (i, 0)),
                  pl.BlockSpec((d, n), lambda i: (0, 0)),
                  pl.BlockSpec((1, n), lambda i: (0, 0)),
                  pl.BlockSpec((tm, 128), lambda i: (i % per_seq, 0)),
                  pl.BlockSpec((tm, 128), lambda i: (i % per_seq, 0))],
        out_specs=pl.BlockSpec((tm, n), lambda i: (i, 0)),
        out_shape=jax.ShapeDtypeStruct((t, n), BF16),
        compiler_params=_params("parallel"),
        name="qkv_rope",
    )(h, w, bias.reshape(1, -1), cos, sin)


Q_COLS = N_Q_HEADS * HEAD_DIM
KV_COLS = 4 * 128
NEG = -1e30


def _attn_kernel(sink_ref, q_ref, kv_prev_ref, kv_cur_ref, o_ref):
    n = pl.program_id(1)
    blk = ATT_BLOCK
    kv = jnp.concatenate([kv_prev_ref[...], kv_cur_ref[...]], axis=0)
    qi = lax.broadcasted_iota(jnp.int32, (blk, 2 * blk), 0)
    kj = lax.broadcasted_iota(jnp.int32, (blk, 2 * blk), 1)
    first_key = FRONT_PAD - (n - 1) * blk
    valid = (kj > qi) & (kj <= qi + blk) & (kj >= first_key)
    lane = lax.broadcasted_iota(jnp.int32, (blk, 128), 1)
    low = lane < HEAD_DIM
    lane_kv = lax.broadcasted_iota(jnp.int32, (2 * blk, 128), 1)
    low_kv = lane_kv < HEAD_DIM
    zero = jnp.zeros((), BF16)
    for pair in range(N_Q_HEADS // 2):
        grp = (2 * pair) // GROUP
        kd = kv[:, grp * 128:(grp + 1) * 128]
        vd = kv[:, (N_KV_HEADS + grp) * 128:(N_KV_HEADS + grp + 1) * 128]
        qp = q_ref[:, pair * 128:(pair + 1) * 128]
        out = None
        for half in range(2):
            head = 2 * pair + half
            keep = low if half == 0 else ~low
            keep_kv = low_kv if half == 0 else ~low_kv
            s = lax.dot_general(jnp.where(keep, qp, zero), kd, NT,
                                preferred_element_type=F32) * (HEAD_DIM ** -0.5)
            s = jnp.where(valid, s, NEG)
            sink = sink_ref[head]
            m = jnp.maximum(jnp.max(s, axis=-1, keepdims=True), sink)
            p = jnp.exp(s - m)
            denom = jnp.sum(p, axis=-1, keepdims=True) + jnp.exp(sink - m)
            p = (p / denom).astype(BF16)
            o = jnp.dot(p, jnp.where(keep_kv, vd, zero), preferred_element_type=F32)
            out = o if out is None else out + o
        o_ref[:, pair * 128:(pair + 1) * 128] = out.astype(o_ref.dtype)


def _attention(qkv, sinks, n_batch, lp):
    t = qkv.shape[0]
    nb = lp // ATT_BLOCK
    q_blocks = Q_COLS // KV_COLS
    return pl.pallas_call(
        _attn_kernel,
        grid=(n_batch, nb),
        in_specs=[pl.BlockSpec(memory_space=pltpu.SMEM),
                  pl.BlockSpec((ATT_BLOCK, Q_COLS), lambda b, n: (b * nb + n, 0)),
                  pl.BlockSpec((ATT_BLOCK, KV_COLS),
                               lambda b, n: (b * nb + jnp.maximum(n - 1, 0), q_blocks)),
                  pl.BlockSpec((ATT_BLOCK, KV_COLS), lambda b, n: (b * nb + n, q_blocks))],
        out_specs=pl.BlockSpec((ATT_BLOCK, Q_COLS), lambda b, n: (b * nb + n, 0)),
        out_shape=jax.ShapeDtypeStruct((t, Q_COLS), BF16),
        compiler_params=_params("parallel", "parallel"),
        name="swa_sink_attention",
    )(sinks.astype(F32), qkv, qkv, qkv)


def _router_kernel(h_ref, w_ref, o_ref):
    logits = _dot(h_ref[...], w_ref[...], NN, passes=3)
    lane = lax.broadcasted_iota(jnp.int32, logits.shape, 1)
    logits = jnp.where(lane < N_EXPERTS, logits, -jnp.inf)
    v1 = jnp.max(logits, axis=-1, keepdims=True)
    i1 = jnp.min(jnp.where(logits == v1, lane, 128), axis=-1, keepdims=True)
    rest = jnp.where(lane == i1, -jnp.inf, logits)
    v2 = jnp.max(rest, axis=-1, keepdims=True)
    i2 = jnp.min(jnp.where(rest == v2, lane, 128), axis=-1, keepdims=True)
    w1 = 1.0 / (1.0 + jnp.exp(v2 - v1))
    w2 = 1.0 - w1
    out = jnp.where(lane == 0, i1.astype(F32), 0.0)
    out = jnp.where(lane == 1, i2.astype(F32), out)
    out = jnp.where(lane == 2, w1, out)
    out = jnp.where(lane == 3, w2, out)
    o_ref[...] = out


def _router(h, router_w, tm):
    t, d = h.shape
    w = jnp.zeros((d, 128), F32).at[:, :N_EXPERTS].set(router_w)
    return pl.pallas_call(
        _router_kernel,
        grid=(t // tm,),
        in_specs=[pl.BlockSpec((tm, d), lambda i: (i, 0)), pl.BlockSpec((d, 128), lambda i: (0, 0))],
        out_specs=pl.BlockSpec((tm, 128), lambda i: (i, 0)),
        out_shape=jax.ShapeDtypeStruct((t, 128), F32),
        compiler_params=_params("parallel"),
        name="router_top2",
    )(h, w)


def _moe_kernel(te_ref, tv_ref, x_ref, wg_ref, wu_ref, wd_ref, o_ref):
    i = pl.program_id(0)
    j = pl.program_id(1)

    @pl.when(tv_ref[i] > 0)
    def _():
        part = _swiglu_step(x_ref[...], wg_ref.at[0], wu_ref.at[0], wd_ref.at[0])

        @pl.when(j == 0)
        def _():
            o_ref[...] = part

        @pl.when(j > 0)
        def _():
            o_ref[...] += part


def _moe_experts(x_sorted, tile_expert, tile_valid, wg, wu, wd, tm, tf):
    n_rows, d = x_sorted.shape
    dff = wg.shape[2]
    grid_spec = pltpu.PrefetchScalarGridSpec(
        num_scalar_prefetch=2,
        grid=(n_rows // tm, dff // tf),
        in_specs=[pl.BlockSpec((tm, d), lambda i, j, te, tv: (i, 0)),
                  pl.BlockSpec((1, d, tf), lambda i, j, te, tv: (te[i], 0, j)),
                  pl.BlockSpec((1, d, tf), lambda i, j, te, tv: (te[i], 0, j)),
                  pl.BlockSpec((1, tf, d), lambda i, j, te, tv: (te[i], j, 0))],
        out_specs=pl.BlockSpec((tm, d), lambda i, j, te, tv: (i, 0)),
    )
    return pl.pallas_call(
        _moe_kernel,
        grid_spec=grid_spec,
        out_shape=jax.ShapeDtypeStruct((n_rows, d), F32),
        compiler_params=_params("parallel", "arbitrary"),
        name="moe_expert_swiglu",
    )(tile_expert, tile_valid, x_sorted, wg, wu, wd)


def _route_plan(top_idx, tm):
    n_assign = top_idx.size
    n_rows = n_assign + N_EXPERTS * tm
    flat = top_idx.reshape(-1)
    onehot = (flat[:, None] == jnp.arange(N_EXPERTS, dtype=jnp.int32)[None, :]).astype(jnp.int32)
    csum = jnp.cumsum(onehot, axis=0)
    rank = jnp.sum(onehot * (csum - 1), axis=1)
    counts = csum[-1]
    padded = ((counts + tm - 1) // tm) * tm
    ends = jnp.cumsum(padded)
    starts = ends - padded
    pos = starts[flat] + rank
    src = jnp.zeros((n_rows,), jnp.int32).at[pos].set(jnp.arange(n_assign, dtype=jnp.int32) // 2)
    tile_start = jnp.arange(n_rows // tm, dtype=jnp.int32) * tm
    tile_expert = jnp.minimum(jnp.searchsorted(ends, tile_start, side="right"),
                              N_EXPERTS - 1).astype(jnp.int32)
    tile_valid = (tile_start < ends[-1]).astype(jnp.int32)
    return pos.reshape(top_idx.shape), src, tile_expert, tile_valid


def _combine_ln_kernel(h_ref, y1_ref, y2_ref, r_ref, g_ref, b_ref, o_ref):
    route = r_ref[...]
    acc = ALPHA * h_ref[...] + route[:, 2:3] * y1_ref[...] + route[:, 3:4] * y2_ref[...]
    o_ref[...] = _layer_norm(acc, g_ref[...], b_ref[...], LN_EPS)


def _combine_ln(h, y1, y2, route, g, b, tm):
    t, d = h.shape
    row = lambda v: v.reshape(1, -1)
    tile = pl.BlockSpec((tm, d), lambda i: (i, 0))
    vec = pl.BlockSpec((1, d), lambda i: (0, 0))
    return pl.pallas_call(
        _combine_ln_kernel,
        grid=(t // tm,),
        in_specs=[tile, tile, tile, pl.BlockSpec((tm, 128), lambda i: (i, 0)), vec, vec],
        out_specs=tile,
        out_shape=jax.ShapeDtypeStruct((t, d), F32),
        compiler_params=_params("parallel"),
        name="moe_combine_ln",
    )(h, y1, y2, route, row(g), row(b))


def _pick_tile(n, target, mult=16):
    best = mult
    for cand in range(mult, min(n, target) + 1, mult):
        if n % cand == 0:
            best = cand
    return best


def kernel(x, meta_tokens, ev_w_in, ev_conv_w, ev_conv_b, ev_convnorm_g, ev_convnorm_b, ev_shift_mu, ev_w0, ev_w2, ev_a0, ev_a2, ev_g2, ev_k_k, ev_k_a, ev_r_k, ev_lnx_g, ev_lnx_b, ev_w_out, ev_ln1_g, ev_ln1_b, ev_ffn_gate, ev_ffn_up, ev_ffn_down, ev_ln2_g, ev_ln2_b, od_w_qkv, od_b_qkv, od_sinks, od_w_o, od_b_o, od_ln1_g, od_ln1_b, od_router, od_exp_gate, od_exp_up, od_exp_down, od_ln2_g, od_ln2_b):
    n_batch, seq, d = x.shape
    lp = FRONT_PAD + N_META + seq
    assert d == D_MODEL and lp % ATT_BLOCK == 0 and lp % CHUNK == 0
    t = n_batch * lp
    tm = _pick_tile(t, 1024)
    tm_seq = _pick_tile(lp, 1056)
    assert t % CONV_ROWS == 0

    meta = jnp.broadcast_to(meta_tokens[None].astype(x.dtype), (n_batch, N_META, d))
    h = jnp.concatenate([jnp.zeros((n_batch, FRONT_PAD, d), x.dtype), meta, x], axis=1)
    h = h.reshape(t, d)

    w_in = ev_w_in[0].astype(BF16)
    u_conv = _matmul(h, w_in[:, :2 * CONV_CH], tm, 1024, F32)
    u_rwkv = _matmul(h, w_in[:, 2 * CONV_CH:], tm, RWKV_COLS // 2, F32)
    a_out = _conv_module(u_conv, ev_conv_w[0], ev_conv_b[0], ev_convnorm_g[0], ev_convnorm_b[0])
    b_out = _rwkv_time_mix(u_rwkv, n_batch, ev_shift_mu[0], ev_w0[0], ev_w2[0], ev_a0[0], ev_a2[0],
                           ev_g2[0], ev_k_k[0], ev_k_a[0], ev_r_k[0], ev_lnx_g[0], ev_lnx_b[0])
    w_out = ev_w_out[0].astype(BF16)
    h = _proj_ln(h, [a_out, b_out], [w_out[:CONV_CH], w_out[CONV_CH:]], None,
                 ev_ln1_g[0], ev_ln1_b[0], tm)
    h = _ffn_ln(h, ev_ffn_gate[0].astype(BF16), ev_ffn_up[0].astype(BF16),
                ev_ffn_down[0].astype(BF16), ev_ln2_g[0], ev_ln2_b[0], tm, 1408)

    wq, wk, wv = jnp.split(od_w_qkv[0], [Q_COLS, Q_COLS + N_KV_HEADS * HEAD_DIM], axis=1)
    bq, bk, bv = jnp.split(od_b_qkv[0], [Q_COLS, Q_COLS + N_KV_HEADS * HEAD_DIM])
    dup_w = lambda w: jnp.concatenate(
        [w[:, g * HEAD_DIM:(g + 1) * HEAD_DIM] for g in range(N_KV_HEADS) for _ in range(2)], axis=1)
    dup_b = lambda v: dup_w(v[None])[0]
    w_qkv = jnp.concatenate([wq, dup_w(wk), dup_w(wv)], axis=1).astype(BF16)
    b_qkv = jnp.concatenate([bq, dup_b(bk), dup_b(bv)])
    qkv = _qkv_rope(h, w_qkv, b_qkv, lp, tm_seq)
    att = _attention(qkv, od_sinks[0], n_batch, lp)
    h = _proj_ln(h, [att], [od_w_o[0].astype(BF16)], od_b_o[0], od_ln1_g[0], od_ln1_b[0], tm)

    route = _router(h, od_router[0], tm)
    top_idx = route[:, :2].astype(jnp.int32)
    tm_moe = 1024 if (2 * t) % 1024 == 0 else _pick_tile(2 * t, 1024)
    pos, src, tile_expert, tile_valid = _route_plan(top_idx, tm_moe)
    x_sorted = jnp.take(h.astype(BF16), src, axis=0)
    y_sorted = _moe_experts(x_sorted, tile_expert, tile_valid, od_exp_gate[0].astype(BF16),
                            od_exp_up[0].astype(BF16), od_exp_down[0].astype(BF16), tm_moe, 896)
    y1 = jnp.take(y_sorted, pos[:, 0], axis=0)
    y2 = jnp.take(y_sorted, pos[:, 1], axis=0)
    h = _combine_ln(h, y1, y2, route, od_ln2_g[0], od_ln2_b[0], tm)

    return h.reshape(n_batch, lp, d)[:, FRONT_PAD + N_META:]
```

```python
import functools
import math

import jax
import jax.numpy as jnp
import numpy as np
from jax import lax
from jax.experimental import pallas as pl
from jax.experimental.pallas import tpu as pltpu

F32 = jnp.float32
BF16 = jnp.bfloat16

D_MODEL = 1024
N_META = 16
ATT_BLOCK = 128
FRONT_PAD = (-N_META) % ATT_BLOCK
CONV_CH = 512
CONV_WIDTH = 31
RWKV_CH = 512
RWKV_HEAD = 64
RWKV_COLS = 3 * RWKV_CH + 64 + 64 + 128
HEAD_DIM = 64
N_Q_HEADS = 16
N_KV_HEADS = 2
GROUP = N_Q_HEADS // N_KV_HEADS
ROPE_THETA = 10000.0
N_EXPERTS = 8
DEPTH = 2
ALPHA = (2 * DEPTH) ** 0.25
LN_EPS = 1e-5
LNX_EPS = 64e-5

CHUNK = 64
RWKV_CHUNKS_PER_STEP = 3
HEADS_PER_GROUP = 2
GROUP_W = HEADS_PER_GROUP * RWKV_HEAD
VMEM_LIMIT = 56 * 1024 * 1024

NN = (((1,), (0,)), ((), ()))
NT = (((1,), (1,)), ((), ()))
TN = (((0,), (0,)), ((), ()))


def _params(*sem):
    return pltpu.CompilerParams(dimension_semantics=sem, vmem_limit_bytes=VMEM_LIMIT)


def _sigmoid(x):
    return 1.0 / (1.0 + jnp.exp(-x))


def _split2(x):
    hi = x.astype(BF16)
    lo = (x - hi.astype(F32)).astype(BF16)
    return hi, lo


def _dot(a, b, dims=NN, passes=1):
    if passes == 1:
        return lax.dot_general(a.astype(BF16), b.astype(BF16), dims, preferred_element_type=F32)
    ah, al = _split2(a)
    bh, bl = _split2(b)
    out = lax.dot_general(ah, bh, dims, preferred_element_type=F32)
    out = out + lax.dot_general(ah, bl, dims, preferred_element_type=F32)
    return out + lax.dot_general(al, bh, dims, preferred_element_type=F32)


def _dot_exact_rhs(a, b_bf16, dims=NN, parts=2):
    out = None
    rem = a
    for _ in range(parts):
        piece = rem.astype(BF16)
        term = lax.dot_general(piece, b_bf16, dims, preferred_element_type=F32)
        out = term if out is None else out + term
        rem = rem - piece.astype(F32)
    return out


def _layer_norm(x, g, b, eps):
    mu = jnp.mean(x, axis=-1, keepdims=True)
    xc = x - mu
    var = jnp.mean(xc * xc, axis=-1, keepdims=True)
    return xc * lax.rsqrt(var + eps) * g + b


def _mm_kernel(x_ref, w_ref, o_ref):
    o_ref[...] = jnp.dot(x_ref[...].astype(BF16), w_ref[...],
                         preferred_element_type=F32).astype(o_ref.dtype)


def _matmul(x, w, tm, tn, out_dtype):
    t, k = x.shape
    n = w.shape[1]
    return pl.pallas_call(
        _mm_kernel,
        grid=(t // tm, n // tn),
        in_specs=[pl.BlockSpec((tm, k), lambda i, j: (i, 0)),
                  pl.BlockSpec((k, tn), lambda i, j: (0, j))],
        out_specs=pl.BlockSpec((tm, tn), lambda i, j: (i, j)),
        out_shape=jax.ShapeDtypeStruct((t, n), out_dtype),
        compiler_params=_params("parallel", "parallel"),
        name="matmul",
    )(x, w)


CONV_ROWS = 256
CONV_SUB = 32
CONV_HALO = 32
SUBLANES = 8


def _conv_kernel(u_ref, cw_ref, cb_ref, g_ref, b_ref, o_ref, buf_ref, sh_ref):
    @pl.when(pl.program_id(0) == 0)
    def _():
        buf_ref[0:CONV_HALO, :] = jnp.zeros((CONV_HALO, CONV_CH), F32)

    u = u_ref[...]
    buf_ref[CONV_HALO:CONV_HALO + CONV_ROWS, :] = u[:, :CONV_CH] * _sigmoid(u[:, CONV_CH:])
    n_sh = CONV_HALO + CONV_ROWS - SUBLANES
    for r in range(1, SUBLANES):
        sh_ref[r - 1] = buf_ref[r:r + n_sh, :]
    cw = cw_ref[...]
    base = CONV_HALO - (CONV_WIDTH - 1)
    for s in range(CONV_ROWS // CONV_SUB):
        r0 = s * CONV_SUB
        acc = jnp.broadcast_to(cb_ref[...], (CONV_SUB, CONV_CH))
        for j in range(CONV_WIDTH):
            shift = (base + j) % SUBLANES
            start = base + j - shift + r0
            if shift == 0:
                win = buf_ref[start:start + CONV_SUB, :]
            else:
                win = sh_ref[shift - 1, start:start + CONV_SUB, :]
            acc = acc + cw[j:j + 1, :] * win
        y = _layer_norm(acc, g_ref[...], b_ref[...], LN_EPS)
        o_ref[r0:r0 + CONV_SUB, :] = (y * _sigmoid(y)).astype(o_ref.dtype)
    buf_ref[0:CONV_HALO, :] = buf_ref[CONV_ROWS:CONV_ROWS + CONV_HALO, :]


def _conv_module(u_conv, conv_w, conv_b, norm_g, norm_b):
    t = u_conv.shape[0]
    row = lambda v: v.reshape(1, -1)
    full = lambda shape: pl.BlockSpec(shape, lambda i: (0, 0))
    return pl.pallas_call(
        _conv_kernel,
        grid=(t // CONV_ROWS,),
        in_specs=[pl.BlockSpec((CONV_ROWS, 2 * CONV_CH), lambda i: (i, 0)),
                  full((CONV_WIDTH, CONV_CH)), full((1, CONV_CH)), full((1, CONV_CH)),
                  full((1, CONV_CH))],
        out_specs=pl.BlockSpec((CONV_ROWS, CONV_CH), lambda i: (i, 0)),
        out_shape=jax.ShapeDtypeStruct((t, CONV_CH), BF16),
        scratch_shapes=[pltpu.VMEM((CONV_ROWS + CONV_HALO, CONV_CH), F32),
                        pltpu.VMEM((SUBLANES - 1, CONV_ROWS + CONV_HALO - SUBLANES, CONV_CH), F32)],
        compiler_params=_params("arbitrary"),
        name="conv_module",
    )(u_conv, conv_w, row(conv_b), row(norm_g), row(norm_b))


def _rwkv_masks(step_rows):
    n = GROUP_W
    r = np.arange(n)[:, None]
    c = np.arange(n)[None, :]
    same = lambda w: (r // w) == (c // w)
    strict = (r > c) & same(CHUNK)
    incl = (r >= c) & same(CHUNK)
    m16 = (r > c) & same(16)
    m32 = (r > c) & same(32) & ~same(16)
    m64 = (r > c) & same(64) & ~same(32)
    stack = (c // RWKV_HEAD) == (r // CHUNK)
    masks = np.stack([strict, incl, m16, m32, m64, stack]).astype(np.float32)
    tr = np.arange(step_rows)
    tri = ((tr[:, None] >= tr[None, :]) & ((tr[:, None] // CHUNK) == (tr[None, :] // CHUNK)))
    tri = tri.astype(np.float32)
    hr = np.arange(RWKV_CH)
    head_ones = ((hr[:, None] // RWKV_HEAD) == (hr[None, :] // RWKV_HEAD)).astype(np.float32)
    return masks, tri, head_ones


def _each(f, *cols):
    return [f(*xs) for xs in zip(*cols)]


def _unit_lower_inverse_minus_identity(labs, m16, m32, m64, passes):
    mm = functools.partial(_dot, passes=passes)
    d = [lab * m16 for lab in labs]
    d2 = _each(mm, d, d)
    d4 = _each(mm, d2, d2)
    dd2 = _each(mm, d, d2)
    d8 = _each(mm, d4, d4)
    n = _each(lambda x, x2, xx2: x2 - x - xx2, d, d2, dd2)
    n = _each(lambda x, y, xy: x + y + xy, n, d4, _each(mm, n, d4))
    n = _each(lambda x, y, xy: x + y + xy, n, d8, _each(mm, n, d8))
    for mask in (m32, m64):
        off = [lab * mask for lab in labs]
        w = _each(lambda o, no: o + no, off, _each(mm, n, off))
        w = _each(lambda x, xn: x + xn, w, _each(mm, w, n))
        n = _each(lambda x, y: x - y, n, w)
    return n


def _rwkv_kernel(u_ref, mu_ref, w0_ref, w2_ref, a0_ref, a2_ref, g2_ref, kk_ref, ka_ref, rk_ref,
                 lg_ref, lb_ref, masks_ref, tri_ref, ones_ref, o_ref, s_ref, carry_ref):
    c = CHUNK
    rows = u_ref.shape[0]
    n_groups = RWKV_CH // GROUP_W
    w = GROUP_W

    @pl.when(pl.program_id(1) == 0)
    def _():
        s_ref[...] = jnp.zeros_like(s_ref)
        carry_ref[...] = jnp.zeros_like(carry_ref)

    z = u_ref[...]
    prev = pltpu.roll(z, 1, 0)
    row_id = lax.broadcasted_iota(jnp.int32, z.shape, 0)
    prev = jnp.where(row_id == 0, jnp.broadcast_to(carry_ref[7:8, :], z.shape), prev)
    carry_ref[...] = z[rows - 8:rows, :]
    z = z + (prev - z) * mu_ref[...]

    r = z[:, 0:RWKV_CH]
    k = z[:, RWKV_CH:2 * RWKV_CH]
    v = z[:, 2 * RWKV_CH:3 * RWKV_CH]
    wa_lo = z[:, 3 * RWKV_CH:3 * RWKV_CH + 128]
    g_lo = z[:, 3 * RWKV_CH + 128:3 * RWKV_CH + 256]

    wpre = w0_ref[...] + _dot(jnp.tanh(wa_lo), w2_ref[...], passes=3)
    ew = _sigmoid(wpre) * math.exp(-0.5)
    a = _sigmoid(a0_ref[...] + _dot(wa_lo, a2_ref[...], passes=3))
    gate = _dot(_sigmoid(g_lo), g2_ref[...], passes=3)

    ones = ones_ref[...]
    kk = k * kk_ref[...]
    ss = _dot_exact_rhs(kk * kk, ones)
    kk = kk * lax.rsqrt(jnp.maximum(ss, 1e-24))
    k2 = k * (1.0 + (a - 1.0) * ka_ref[...])
    kka = kk * a

    e0 = ew.astype(BF16)
    e1f = ew - e0.astype(F32)
    e1 = e1f.astype(BF16)
    e2 = (e1f - e1.astype(F32)).astype(BF16)
    tri = tri_ref[...]
    cum = (jnp.dot(tri, e0, preferred_element_type=F32) + jnp.dot(tri, e1, preferred_element_type=F32)
           + jnp.dot(tri, e2, preferred_element_type=F32))
    gam = jnp.exp(-cum)
    igam = jnp.exp(cum)
    r_t = r * gam
    ka_t = kk * jnp.exp(ew - cum)
    b_t = kka * igam
    k_t = k2 * igam

    strict, incl, m16, m32, m64, stack = (masks_ref[i] for i in range(6))
    p1 = 1
    mm = functools.partial(_dot, passes=p1)

    units = [(q, g) for q in range(rows // c) for g in range(n_groups)]

    def stacked(x):
        return [jnp.concatenate([x[q * c:(q + 1) * c, g * w:(g + 1) * w]] * HEADS_PER_GROUP, axis=0)
                * stack for q, g in units]

    xa, xr, xv, yb, yk = stacked(ka_t), stacked(r_t), stacked(v), stacked(b_t), stacked(k_t)
    xar = _each(lambda top, bot: jnp.concatenate([top, bot], axis=0), xa, xr)
    pb = _each(lambda x, y: mm(x, y, NT), xar, yb)
    pk = _each(lambda x, y: mm(x, y, NT), xar, yk)
    lab = [p[:w] * strict for p in pb]
    arb = [p[w:] * incl for p in pb]
    lak = [p[:w] * strict for p in pk]
    ark = [p[w:] * incl for p in pk]
    n_inv = _unit_lower_inverse_minus_identity(lab, m16, m32, m64, p1)
    lv = _each(mm, lak, xv)
    rhs = _each(lambda left, right: jnp.concatenate([left, right], axis=1), xa, lv)
    gz = _each(lambda x, nx: x + nx, rhs, _each(mm, n_inv, rhs))
    corr = _each(mm, arb, gz)
    qh = _each(lambda x, cr: x - cr[:, :w], xr, corr)
    yloc = _each(lambda av, cr: av - cr[:, w:], _each(mm, ark, xv), corr)
    gzb = _each(lambda x, y: mm(x, y, TN), gz, yb)
    vk = _each(lambda x, y: mm(x, y, TN), xv, yk)

    state = [s_ref[g] for g in range(n_groups)]
    y_rows = []
    for q in range(rows // c):
        gam_end = gam[(q + 1) * c - 1:(q + 1) * c, :]
        ys = []
        for g in range(n_groups):
            i = q * n_groups + g
            s0 = state[g]
            y_st = mm(qh[i], s0, NT) + yloc[i]
            y_heads = y_st[0:c]
            for h in range(1, HEADS_PER_GROUP):
                y_heads = y_heads + y_st[h * c:(h + 1) * c]
            ys.append(y_heads)
            s_new = s0 - mm(s0, gzb[i][:w]) + vk[i] - gzb[i][w:]
            state[g] = s_new * gam_end[:, g * w:(g + 1) * w]
        y_rows.append(jnp.concatenate(ys, axis=1))
    for g in range(n_groups):
        s_ref[g] = state[g]
    y = jnp.concatenate(y_rows, axis=0)

    inv_n = 1.0 / RWKV_HEAD
    mean = _dot_exact_rhs(y, ones) * inv_n
    yc = y - mean
    var = _dot_exact_rhs(yc * yc, ones) * inv_n
    yn = yc * lax.rsqrt(var + LNX_EPS) * lg_ref[...] + lb_ref[...]
    bonus = _dot_exact_rhs(r * k2 * rk_ref[...], ones) * v
    o_ref[...] = ((yn + bonus) * gate).astype(o_ref.dtype)


def _rwkv_time_mix(u_rwkv, n_batch, shift_mu, w0, w2, a0, a2, g2, k_k, k_a, r_k, lnx_g, lnx_b):
    t = u_rwkv.shape[0]
    step_rows = CHUNK * RWKV_CHUNKS_PER_STEP
    n_steps = t // n_batch // step_rows
    masks, tri, head_ones = _rwkv_masks(step_rows)
    row = lambda v: v.reshape(1, -1).astype(F32)
    zeros64 = jnp.zeros((64, RWKV_CH), F32)
    w2p = jnp.concatenate([w2, zeros64], axis=0)
    a2p = jnp.concatenate([zeros64, a2], axis=0)
    full2 = lambda shape: pl.BlockSpec(shape, lambda b, c: (0, 0))
    vec = full2((1, RWKV_CH))
    return pl.pallas_call(
        _rwkv_kernel,
        grid=(n_batch, n_steps),
        in_specs=[pl.BlockSpec((step_rows, RWKV_COLS), lambda b, c: (b * n_steps + c, 0)),
                  full2((1, RWKV_COLS)), vec, full2((128, RWKV_CH)), vec, full2((128, RWKV_CH)),
                  full2((128, RWKV_CH)), vec, vec, vec, vec, vec,
                  pl.BlockSpec(masks.shape, lambda b, c: (0, 0, 0)),
                  full2((step_rows, step_rows)), full2((RWKV_CH, RWKV_CH))],
        out_specs=pl.BlockSpec((step_rows, RWKV_CH), lambda b, c: (b * n_steps + c, 0)),
        out_shape=jax.ShapeDtypeStruct((t, RWKV_CH), BF16),
        scratch_shapes=[pltpu.VMEM((RWKV_CH // GROUP_W, GROUP_W, GROUP_W), F32),
                        pltpu.VMEM((8, RWKV_COLS), F32)],
        compiler_params=_params("arbitrary", "arbitrary"),
        name="rwkv7_time_mix",
    )(u_rwkv, row(shift_mu), row(w0), w2p, row(a0), a2p, g2, row(k_k), row(k_a), row(r_k),
      row(lnx_g), row(lnx_b), jnp.asarray(masks), jnp.asarray(tri, BF16),
      jnp.asarray(head_ones, BF16))


def _proj_ln_kernel(n_in, has_bias, *refs):
    h_ref = refs[0]
    xs = refs[1:1 + n_in]
    ws = refs[1 + n_in:1 + 2 * n_in]
    rest = refs[1 + 2 * n_in:]
    if has_bias:
        bias_ref, g_ref, b_ref, o_ref = rest
    else:
        g_ref, b_ref, o_ref = rest
    acc = ALPHA * h_ref[...]
    for x_ref, w_ref in zip(xs, ws):
        acc = acc + jnp.dot(x_ref[...].astype(BF16), w_ref[...], preferred_element_type=F32)
    if has_bias:
        acc = acc + bias_ref[...]
    o_ref[...] = _layer_norm(acc, g_ref[...], b_ref[...], LN_EPS)


def _proj_ln(h, xs, ws, bias, g, b, tm):
    t, d = h.shape
    row = lambda v: v.reshape(1, -1)
    full = lambda shape: pl.BlockSpec(shape, lambda i: (0, 0))
    in_specs = [pl.BlockSpec((tm, d), lambda i: (i, 0))]
    in_specs += [pl.BlockSpec((tm, x.shape[1]), lambda i: (i, 0)) for x in xs]
    in_specs += [full(w.shape) for w in ws]
    args = [h, *xs, *ws]
    if bias is not None:
        in_specs.append(full((1, d)))
        args.append(row(bias))
    in_specs += [full((1, d)), full((1, d))]
    args += [row(g), row(b)]
    return pl.pallas_call(
        functools.partial(_proj_ln_kernel, len(xs), bias is not None),
        grid=(t // tm,),
        in_specs=in_specs,
        out_specs=pl.BlockSpec((tm, d), lambda i: (i, 0)),
        out_shape=jax.ShapeDtypeStruct((t, d), F32),
        compiler_params=_params("parallel"),
        name="proj_residual_ln",
    )(*args)


SWIGLU_ROWS = 256


def _swiglu_rows(x_ref, wg_ref, wu_ref, wd_ref, rows):
    xb = x_ref[rows, :].astype(BF16)
    gate = jnp.dot(xb, wg_ref[...], preferred_element_type=F32)
    up = jnp.dot(xb, wu_ref[...], preferred_element_type=F32)
    act = (gate * _sigmoid(gate) * up).astype(BF16)
    return jnp.dot(act, wd_ref[...], preferred_element_type=F32)


def _row_blocks(n_rows):
    return [slice(r0, r0 + SWIGLU_ROWS) for r0 in range(0, n_rows, SWIGLU_ROWS)]


def _ffn_ln_kernel(h_ref, wg_ref, wu_ref, wd_ref, g_ref, b_ref, o_ref, acc_ref):
    j = pl.program_id(1)

    @pl.when(j == 0)
    def _():
        for rows in _row_blocks(h_ref.shape[0]):
            acc_ref[rows, :] = ALPHA * h_ref[rows, :] + _swiglu_rows(h_ref, wg_ref, wu_ref, wd_ref, rows)

    @pl.when(j > 0)
    def _():
        for rows in _row_blocks(h_ref.shape[0]):
            acc_ref[rows, :] += _swiglu_rows(h_ref, wg_ref, wu_ref, wd_ref, rows)

    @pl.when(j == pl.num_programs(1) - 1)
    def _():
        o_ref[...] = _layer_norm(acc_ref[...], g_ref[...], b_ref[...], LN_EPS)


def _ffn_ln(h, wg, wu, wd, g, b, tm, tf):
    t, d = h.shape
    dff = wg.shape[1]
    row = lambda v: v.reshape(1, -1)
    return pl.pallas_call(
        _ffn_ln_kernel,
        grid=(t // tm, dff // tf),
        in_specs=[pl.BlockSpec((tm, d), lambda i, j: (i, 0)),
                  pl.BlockSpec((d, tf), lambda i, j: (0, j)),
                  pl.BlockSpec((d, tf), lambda i, j: (0, j)),
                  pl.BlockSpec((tf, d), lambda i, j: (j, 0)),
                  pl.BlockSpec((1, d), lambda i, j: (0, 0)),
                  pl.BlockSpec((1, d), lambda i, j: (0, 0))],
        out_specs=pl.BlockSpec((tm, d), lambda i, j: (i, 0)),
        out_shape=jax.ShapeDtypeStruct((t, d), F32),
        scratch_shapes=[pltpu.VMEM((tm, d), F32)],
        compiler_params=_params("parallel", "arbitrary"),
        name="swiglu_residual_ln",
    )(h, wg, wu, wd, row(g), row(b))


Q_COLS = N_Q_HEADS * HEAD_DIM
K_COLS = 2 * N_KV_HEADS * 128
V_COLS = N_KV_HEADS * 128
KV_COLS = K_COLS + V_COLS


def _qkv_rope_kernel(h_ref, w_ref, bias_ref, cos_ref, sin_ref, q_ref, kv_ref):
    acc = jnp.dot(h_ref[...].astype(BF16), w_ref[...], preferred_element_type=F32) + bias_ref[...]
    cos = cos_ref[...]
    sin = sin_ref[...]
    lane = lax.broadcasted_iota(jnp.int32, cos.shape, 1)
    first_half = (lane % HEAD_DIM) < (HEAD_DIM // 2)
    for c0 in range(0, Q_COLS + K_COLS, 128):
        x = acc[:, c0:c0 + 128]
        swapped = jnp.where(first_half, pltpu.roll(x, 128 - HEAD_DIM // 2, 1),
                            pltpu.roll(x, HEAD_DIM // 2, 1))
        y = (x * cos + swapped * sin).astype(BF16)
        if c0 < Q_COLS:
            q_ref[:, c0:c0 + 128] = y
        else:
            kv_ref[:, c0 - Q_COLS:c0 - Q_COLS + 128] = y
    kv_ref[:, K_COLS:] = acc[:, Q_COLS + K_COLS:].astype(BF16)


def _rope_tables(lp):
    half = HEAD_DIM // 2
    inv = ROPE_THETA ** (-jnp.arange(half, dtype=F32) / half)
    pos = (jnp.arange(lp) - FRONT_PAD).astype(F32)
    ang = pos[:, None] * inv[None, :]
    cos = jnp.cos(ang)
    sin = jnp.sin(ang)
    cos = jnp.concatenate([cos, cos] * (128 // HEAD_DIM), axis=1)
    sin = jnp.concatenate([-sin, sin] * (128 // HEAD_DIM), axis=1)
    return cos, sin


def _qkv_rope(h, w, bias, lp, tm):
    t, d = h.shape
    n = w.shape[1]
    cos, sin = _rope_tables(lp)
    per_seq = lp // tm
    return pl.pallas_call(
        _qkv_rope_kernel,
        grid=(t // tm,),
        in_specs=[pl.BlockSpec((tm, d), lambda i: (i, 0)),
                  pl.BlockSpec((d, n), lambda i: (0, 0)),
                  pl.BlockSpec((1, n), lambda i: (0, 0)),
                  pl.BlockSpec((tm, 128), lambda i: (i % per_seq, 0)),
                  pl.BlockSpec((tm, 128), lambda i: (i % per_seq, 0))],
        out_specs=[pl.BlockSpec((tm, Q_COLS), lambda i: (i, 0)),
                   pl.BlockSpec((tm, KV_COLS), lambda i: (i, 0))],
        out_shape=[jax.ShapeDtypeStruct((t, Q_COLS), BF16), jax.ShapeDtypeStruct((t, KV_COLS), BF16)],
        compiler_params=_params("parallel"),
        name="qkv_rope",
    )(h, w, bias.reshape(1, -1), cos, sin)


def _qkv_weights(w_qkv, b_qkv):
    kv_w = N_KV_HEADS * HEAD_DIM
    wq, wk, wv = jnp.split(w_qkv, [Q_COLS, Q_COLS + kv_w], axis=1)
    bq, bk, bv = jnp.split(b_qkv[None], [Q_COLS, Q_COLS + kv_w], axis=1)

    def expand(m):
        zero = jnp.zeros((m.shape[0], HEAD_DIM), m.dtype)
        k_cols, v_cols = [], []
        for g in range(N_KV_HEADS):
            col = m[:, g * HEAD_DIM:(g + 1) * HEAD_DIM]
            k_cols += [col, zero, zero, col]
            v_cols += [col, col]
        return k_cols, v_cols

    w_cols = [wq] + expand(wk)[0] + expand(wv)[1]
    b_cols = [bq] + expand(bk)[0] + expand(bv)[1]
    return jnp.concatenate(w_cols, axis=1), jnp.concatenate(b_cols, axis=1)[0]


NEG = -1e30


def _attn_kernel(sink_ref, q_ref, kv_prev_ref, kv_cur_ref, o_ref):
    n = pl.program_id(1)
    blk = ATT_BLOCK
    kv = jnp.concatenate([kv_prev_ref[...], kv_cur_ref[...]], axis=0)
    qi = lax.broadcasted_iota(jnp.int32, (blk, 2 * blk), 0)
    kj = lax.broadcasted_iota(jnp.int32, (blk, 2 * blk), 1)
    first_key = FRONT_PAD - (n - 1) * blk
    valid = (kj > qi) & (kj <= qi + blk) & (kj >= first_key)
    low = lax.broadcasted_iota(jnp.int32, (blk, 128), 1) < HEAD_DIM
    for pair in range(N_Q_HEADS // 2):
        grp = (2 * pair) // GROUP
        vd = kv[:, K_COLS + grp * 128:K_COLS + (grp + 1) * 128]
        qp = q_ref[:, pair * 128:(pair + 1) * 128]
        outs = []
        for half in range(2):
            head = 2 * pair + half
            kh = kv[:, (2 * grp + half) * 128:(2 * grp + half + 1) * 128]
            s = lax.dot_general(qp, kh, NT, preferred_element_type=F32) * (HEAD_DIM ** -0.5)
            s = jnp.where(valid, s, NEG)
            sink = sink_ref[head]
            m = jnp.maximum(jnp.max(s, axis=-1, keepdims=True), sink)
            p = jnp.exp(s - m)
            denom = jnp.sum(p, axis=-1, keepdims=True) + jnp.exp(sink - m)
            o = jnp.dot(p.astype(BF16), vd, preferred_element_type=F32)
            outs.append(o * (1.0 / denom))
        o_ref[:, pair * 128:(pair + 1) * 128] = jnp.where(low, outs[0], outs[1]).astype(o_ref.dtype)


def _attention(q, kv, sinks, n_batch, lp):
    t = q.shape[0]
    nb = lp // ATT_BLOCK
    return pl.pallas_call(
        _attn_kernel,
        grid=(n_batch, nb),
        in_specs=[pl.BlockSpec(memory_space=pltpu.SMEM),
                  pl.BlockSpec((ATT_BLOCK, Q_COLS), lambda b, n: (b * nb + n, 0)),
                  pl.BlockSpec((ATT_BLOCK, KV_COLS), lambda b, n: (b * nb + jnp.maximum(n - 1, 0), 0)),
                  pl.BlockSpec((ATT_BLOCK, KV_COLS), lambda b, n: (b * nb + n, 0))],
        out_specs=pl.BlockSpec((ATT_BLOCK, Q_COLS), lambda b, n: (b * nb + n, 0)),
        out_shape=jax.ShapeDtypeStruct((t, Q_COLS), BF16),
        compiler_params=_params("parallel", "parallel"),
        name="swa_sink_attention",
    )(sinks.astype(F32), q, kv, kv)


LANE_E0, LANE_E1, LANE_W0, LANE_W1, LANE_R0, LANE_R1 = range(6)


def _oproj_ln_route_kernel(h_ref, x_ref, w_ref, bias_ref, g_ref, b_ref, rw_ref, tri_ref,
                           o_ref, ob_ref, route_ref, count_ref, run_ref):
    @pl.when(pl.program_id(0) == 0)
    def _():
        run_ref[...] = jnp.zeros_like(run_ref)

    acc = ALPHA * h_ref[...] + bias_ref[...] + jnp.dot(
        x_ref[...], w_ref[...].astype(BF16), preferred_element_type=F32)
    hn = _layer_norm(acc, g_ref[...], b_ref[...], LN_EPS)
    o_ref[...] = hn
    ob_ref[...] = hn.astype(BF16)

    logits = _dot(hn, rw_ref[...], NN, passes=3)
    lane = lax.broadcasted_iota(jnp.int32, logits.shape, 1)
    logits = jnp.where(lane < N_EXPERTS, logits, -jnp.inf)
    v1 = jnp.max(logits, axis=-1, keepdims=True)
    i1 = jnp.min(jnp.where(logits == v1, lane, 128), axis=-1, keepdims=True)
    rest = jnp.where(lane == i1, -jnp.inf, logits)
    v2 = jnp.max(rest, axis=-1, keepdims=True)
    i2 = jnp.min(jnp.where(rest == v2, lane, 128), axis=-1, keepdims=True)
    w1 = 1.0 / (1.0 + jnp.exp(v2 - v1))
    w2 = 1.0 - w1

    oh1 = (lane == i1).astype(F32)
    oh2 = (lane == i2).astype(F32)
    both = oh1 + oh2
    before = jnp.dot(tri_ref[...], both.astype(BF16), preferred_element_type=F32) + run_ref[0:1, :]
    r1 = jnp.sum(oh1 * before, axis=-1, keepdims=True)
    r2 = jnp.sum(oh2 * before, axis=-1, keepdims=True)
    run_ref[...] = run_ref[...] + jnp.sum(both, axis=0, keepdims=True)
    count_ref[...] = run_ref[...]

    out = jnp.where(lane == LANE_E0, i1.astype(F32), 0.0)
    out = jnp.where(lane == LANE_E1, i2.astype(F32), out)
    out = jnp.where(lane == LANE_W0, w1, out)
    out = jnp.where(lane == LANE_W1, w2, out)
    out = jnp.where(lane == LANE_R0, r1, out)
    out = jnp.where(lane == LANE_R1, r2, out)
    route_ref[...] = out


def _oproj_ln_route(h, x, w, bias, g, b, router_w, tm):
    t, d = h.shape
    row = lambda v: v.reshape(1, -1)
    rw = jnp.zeros((d, 128), F32).at[:, :N_EXPERTS].set(router_w)
    tri = jnp.asarray(np.tril(np.ones((tm, tm), np.float32), -1), BF16)
    tile = lambda width: pl.BlockSpec((tm, width), lambda i: (i, 0))
    full = lambda shape: pl.BlockSpec(shape, lambda i: (0, 0))
    return pl.pallas_call(
        _oproj_ln_route_kernel,
        grid=(t // tm,),
        in_specs=[tile(d), tile(x.shape[1]), full(w.shape), full((1, d)), full((1, d)), full((1, d)),
                  full((d, 128)), full((tm, tm))],
        out_specs=[tile(d), tile(d), tile(128), full((8, 128))],
        out_shape=[jax.ShapeDtypeStruct((t, d), F32), jax.ShapeDtypeStruct((t, d), BF16),
                   jax.ShapeDtypeStruct((t, 128), F32), jax.ShapeDtypeStruct((8, 128), F32)],
        scratch_shapes=[pltpu.VMEM((8, 128), F32)],
        compiler_params=_params("arbitrary"),
        name="oproj_ln_route",
    )(h, x, w, row(bias), row(g), row(b), rw, tri)


def _moe_kernel(te_ref, tv_ref, x_ref, wg_ref, wu_ref, wd_ref, o_ref, wg_s, wu_s, wd_s):
    i = pl.program_id(0)
    j = pl.program_id(1)

    @pl.when(tv_ref[i] > 0)
    def _():
        wg_s[...] = wg_ref[0].astype(BF16)
        wu_s[...] = wu_ref[0].astype(BF16)
        wd_s[...] = wd_ref[0].astype(BF16)

        @pl.when(j == 0)
        def _():
            for rows in _row_blocks(x_ref.shape[0]):
                o_ref[rows, :] = _swiglu_rows(x_ref, wg_s, wu_s, wd_s, rows)

        @pl.when(j > 0)
        def _():
            for rows in _row_blocks(x_ref.shape[0]):
                o_ref[rows, :] += _swiglu_rows(x_ref, wg_s, wu_s, wd_s, rows)


def _moe_experts(x_sorted, tile_expert, tile_valid, wg, wu, wd, tm, tf):
    n_rows, d = x_sorted.shape
    dff = wg.shape[2]
    grid_spec = pltpu.PrefetchScalarGridSpec(
        num_scalar_prefetch=2,
        grid=(n_rows // tm, dff // tf),
        in_specs=[pl.BlockSpec((tm, d), lambda i, j, te, tv: (i, 0)),
                  pl.BlockSpec((1, d, tf), lambda i, j, te, tv: (te[i], 0, j)),
                  pl.BlockSpec((1, d, tf), lambda i, j, te, tv: (te[i], 0, j)),
                  pl.BlockSpec((1, tf, d), lambda i, j, te, tv: (te[i], j, 0))],
        out_specs=pl.BlockSpec((tm, d), lambda i, j, te, tv: (i, 0)),
        scratch_shapes=[pltpu.VMEM((d, tf), BF16), pltpu.VMEM((d, tf), BF16),
                        pltpu.VMEM((tf, d), BF16)],
    )
    return pl.pallas_call(
        _moe_kernel,
        grid_spec=grid_spec,
        out_shape=jax.ShapeDtypeStruct((n_rows, d), F32),
        compiler_params=_params("parallel", "arbitrary"),
        name="moe_expert_swiglu",
    )(tile_expert, tile_valid, x_sorted, wg, wu, wd)


def _route_plan(route, counts, tm):
    top_idx = route[:, LANE_E0:LANE_E1 + 1].astype(jnp.int32)
    rank = route[:, LANE_R0:LANE_R1 + 1].astype(jnp.int32)
    n_assign = top_idx.size
    n_rows = n_assign + N_EXPERTS * tm
    counts = counts[0, :N_EXPERTS].astype(jnp.int32)
    padded = ((counts + tm - 1) // tm) * tm
    ends = jnp.cumsum(padded)
    starts = ends - padded
    pos = jnp.take(starts, top_idx) + rank
    token = jnp.arange(n_assign, dtype=jnp.int32) // 2
    src = jnp.zeros((n_rows,), jnp.int32).at[pos.reshape(-1)].set(token)
    tile_start = jnp.arange(n_rows // tm, dtype=jnp.int32) * tm
    tile_expert = jnp.sum((tile_start[:, None] >= ends[None, :]).astype(jnp.int32), axis=1)
    tile_expert = jnp.minimum(tile_expert, N_EXPERTS - 1)
    tile_valid = (tile_start < ends[-1]).astype(jnp.int32)
    return pos, src, tile_expert, tile_valid


def _combine_ln_kernel(h_ref, y1_ref, y2_ref, r_ref, g_ref, b_ref, o_ref):
    route = r_ref[...]
    acc = (ALPHA * h_ref[...] + route[:, LANE_W0:LANE_W0 + 1] * y1_ref[...]
           + route[:, LANE_W1:LANE_W1 + 1] * y2_ref[...])
    o_ref[...] = _layer_norm(acc, g_ref[...], b_ref[...], LN_EPS)


def _combine_ln(h, y1, y2, route, g, b, tm):
    t, d = h.shape
    row = lambda v: v.reshape(1, -1)
    tile = pl.BlockSpec((tm, d), lambda i: (i, 0))
    vec = pl.BlockSpec((1, d), lambda i: (0, 0))
    return pl.pallas_call(
        _combine_ln_kernel,
        grid=(t // tm,),
        in_specs=[tile, tile, tile, pl.BlockSpec((tm, 128), lambda i: (i, 0)), vec, vec],
        out_specs=tile,
        out_shape=jax.ShapeDtypeStruct((t, d), F32),
        compiler_params=_params("parallel"),
        name="moe_combine_ln",
    )(h, y1, y2, route, row(g), row(b))


def _pick_tile(n, target, mult=16):
    best = mult
    for cand in range(mult, min(n, target) + 1, mult):
        if n % cand == 0:
            best = cand
    return best


def kernel(x, meta_tokens, ev_w_in, ev_conv_w, ev_conv_b, ev_convnorm_g, ev_convnorm_b, ev_shift_mu, ev_w0, ev_w2, ev_a0, ev_a2, ev_g2, ev_k_k, ev_k_a, ev_r_k, ev_lnx_g, ev_lnx_b, ev_w_out, ev_ln1_g, ev_ln1_b, ev_ffn_gate, ev_ffn_up, ev_ffn_down, ev_ln2_g, ev_ln2_b, od_w_qkv, od_b_qkv, od_sinks, od_w_o, od_b_o, od_ln1_g, od_ln1_b, od_router, od_exp_gate, od_exp_up, od_exp_down, od_ln2_g, od_ln2_b):
    n_batch, seq, d = x.shape
    lp = FRONT_PAD + N_META + seq
    assert d == D_MODEL and lp % ATT_BLOCK == 0 and lp % (CHUNK * RWKV_CHUNKS_PER_STEP) == 0
    t = n_batch * lp
    tm = _pick_tile(t, 1024)
    tm_seq = _pick_tile(lp, 1056)
    assert t % CONV_ROWS == 0

    meta = jnp.broadcast_to(meta_tokens[None].astype(x.dtype), (n_batch, N_META, d))
    h = jnp.concatenate([jnp.zeros((n_batch, FRONT_PAD, d), x.dtype), meta, x], axis=1)
    h = h.reshape(t, d)

    w_in = ev_w_in[0].astype(BF16)
    u_conv = _matmul(h, w_in[:, :2 * CONV_CH], tm, 1024, F32)
    u_rwkv = _matmul(h, w_in[:, 2 * CONV_CH:], tm, RWKV_COLS // 2, F32)
    a_out = _conv_module(u_conv, ev_conv_w[0], ev_conv_b[0], ev_convnorm_g[0], ev_convnorm_b[0])
    b_out = _rwkv_time_mix(u_rwkv, n_batch, ev_shift_mu[0], ev_w0[0], ev_w2[0], ev_a0[0], ev_a2[0],
                           ev_g2[0], ev_k_k[0], ev_k_a[0], ev_r_k[0], ev_lnx_g[0], ev_lnx_b[0])
    w_out = ev_w_out[0].astype(BF16)
    h = _proj_ln(h, [a_out, b_out], [w_out[:CONV_CH], w_out[CONV_CH:]], None,
                 ev_ln1_g[0], ev_ln1_b[0], tm)
    h = _ffn_ln(h, ev_ffn_gate[0].astype(BF16), ev_ffn_up[0].astype(BF16),
                ev_ffn_down[0].astype(BF16), ev_ln2_g[0], ev_ln2_b[0], tm, 1408)

    w_qkv, b_qkv = _qkv_weights(od_w_qkv[0], od_b_qkv[0])
    q, kv = _qkv_rope(h, w_qkv.astype(BF16), b_qkv, lp, tm_seq)
    att = _attention(q, kv, od_sinks[0], n_batch, lp)
    h, h_bf16, route, counts = _oproj_ln_route(h, att, od_w_o[0], od_b_o[0], od_ln1_g[0], od_ln1_b[0],
                                               od_router[0], tm)
    tm_moe = 1024 if (2 * t) % 1024 == 0 else _pick_tile(2 * t, 1024)
    pos, src, tile_expert, tile_valid = _route_plan(route, counts, tm_moe)
    x_sorted = jnp.take(h_bf16, src, axis=0)
    y_sorted = _moe_experts(x_sorted, tile_expert, tile_valid, od_exp_gate[0], od_exp_up[0],
                            od_exp_down[0], tm_moe, 896)
    y1 = jnp.take(y_sorted, pos[:, 0], axis=0)
    y2 = jnp.take(y_sorted, pos[:, 1], axis=0)
    h = _combine_ln(h, y1, y2, route, od_ln2_g[0], od_ln2_b[0], tm)

    return h.reshape(n_batch, lp, d)[:, FRONT_PAD + N_META:]
```

```python
import functools
import math

import jax
import jax.numpy as jnp
import numpy as np
from jax import lax
from jax.experimental import pallas as pl
from jax.experimental.pallas import tpu as pltpu

F32 = jnp.float32
BF16 = jnp.bfloat16

D_MODEL = 1024
N_META = 16
ATT_BLOCK = 128
FRONT_PAD = (-N_META) % ATT_BLOCK
CONV_CH = 512
CONV_WIDTH = 31
RWKV_CH = 512
RWKV_HEAD = 64
RWKV_COLS = 3 * RWKV_CH + 64 + 64 + 128
HEAD_DIM = 64
N_Q_HEADS = 16
N_KV_HEADS = 2
GROUP = N_Q_HEADS // N_KV_HEADS
ROPE_THETA = 10000.0
N_EXPERTS = 8
DEPTH = 2
ALPHA = (2 * DEPTH) ** 0.25
LN_EPS = 1e-5
LNX_EPS = 64e-5

CHUNK = 64
RWKV_CHUNKS_PER_STEP = 3
HEADS_PER_GROUP = 2
GROUP_W = HEADS_PER_GROUP * RWKV_HEAD
VMEM_LIMIT = 56 * 1024 * 1024

NN = (((1,), (0,)), ((), ()))
NT = (((1,), (1,)), ((), ()))
TN = (((0,), (0,)), ((), ()))


def _params(*sem):
    return pltpu.CompilerParams(dimension_semantics=sem, vmem_limit_bytes=VMEM_LIMIT)


def _sigmoid(x):
    return 1.0 / (1.0 + jnp.exp(-x))


def _split2(x):
    hi = x.astype(BF16)
    lo = (x - hi.astype(F32)).astype(BF16)
    return hi, lo


def _dot(a, b, dims=NN, passes=1):
    if passes == 1:
        return lax.dot_general(a.astype(BF16), b.astype(BF16), dims, preferred_element_type=F32)
    ah, al = _split2(a)
    bh, bl = _split2(b)
    out = lax.dot_general(ah, bh, dims, preferred_element_type=F32)
    out = out + lax.dot_general(ah, bl, dims, preferred_element_type=F32)
    return out + lax.dot_general(al, bh, dims, preferred_element_type=F32)


def _dot_exact_rhs(a, b_bf16, dims=NN, parts=2):
    out = None
    rem = a
    for _ in range(parts):
        piece = rem.astype(BF16)
        term = lax.dot_general(piece, b_bf16, dims, preferred_element_type=F32)
        out = term if out is None else out + term
        rem = rem - piece.astype(F32)
    return out


def _layer_norm(x, g, b, eps):
    mu = jnp.mean(x, axis=-1, keepdims=True)
    xc = x - mu
    var = jnp.mean(xc * xc, axis=-1, keepdims=True)
    return xc * lax.rsqrt(var + eps) * g + b


def _mm_kernel(x_ref, w_ref, o_ref):
    o_ref[...] = jnp.dot(x_ref[...].astype(BF16), w_ref[...],
                         preferred_element_type=F32).astype(o_ref.dtype)


def _matmul(x, w, tm, tn, out_dtype):
    t, k = x.shape
    n = w.shape[1]
    return pl.pallas_call(
        _mm_kernel,
        grid=(t // tm, n // tn),
        in_specs=[pl.BlockSpec((tm, k), lambda i, j: (i, 0)),
                  pl.BlockSpec((k, tn), lambda i, j: (0, j))],
        out_specs=pl.BlockSpec((tm, tn), lambda i, j: (i, j)),
        out_shape=jax.ShapeDtypeStruct((t, n), out_dtype),
        compiler_params=_params("parallel", "parallel"),
        name="matmul",
    )(x, w)


CONV_ROWS = 256
CONV_SUB = 32
CONV_HALO = 32
SUBLANES = 8


def _conv_kernel(h_ref, w_ref, cw_ref, cb_ref, g_ref, b_ref, o_ref, buf_ref, sh_ref):
    @pl.when(pl.program_id(0) == 0)
    def _():
        buf_ref[0:CONV_HALO, :] = jnp.zeros((CONV_HALO, CONV_CH), F32)

    u = jnp.dot(h_ref[...].astype(BF16), w_ref[...], preferred_element_type=F32)
    buf_ref[CONV_HALO:CONV_HALO + CONV_ROWS, :] = u[:, :CONV_CH] * _sigmoid(u[:, CONV_CH:])
    n_sh = CONV_HALO + CONV_ROWS - SUBLANES
    for r in range(1, SUBLANES):
        sh_ref[r - 1] = buf_ref[r:r + n_sh, :]
    cw = cw_ref[...]
    base = CONV_HALO - (CONV_WIDTH - 1)
    for s in range(CONV_ROWS // CONV_SUB):
        r0 = s * CONV_SUB
        acc = jnp.broadcast_to(cb_ref[...], (CONV_SUB, CONV_CH))
        for j in range(CONV_WIDTH):
            shift = (base + j) % SUBLANES
            start = base + j - shift + r0
            if shift == 0:
                win = buf_ref[start:start + CONV_SUB, :]
            else:
                win = sh_ref[shift - 1, start:start + CONV_SUB, :]
            acc = acc + cw[j:j + 1, :] * win
        y = _layer_norm(acc, g_ref[...], b_ref[...], LN_EPS)
        o_ref[r0:r0 + CONV_SUB, :] = (y * _sigmoid(y)).astype(o_ref.dtype)
    buf_ref[0:CONV_HALO, :] = buf_ref[CONV_ROWS:CONV_ROWS + CONV_HALO, :]


def _conv_module(h, w_in_conv, conv_w, conv_b, norm_g, norm_b):
    t, d = h.shape
    row = lambda v: v.reshape(1, -1)
    full = lambda shape: pl.BlockSpec(shape, lambda i: (0, 0))
    return pl.pallas_call(
        _conv_kernel,
        grid=(t // CONV_ROWS,),
        in_specs=[pl.BlockSpec((CONV_ROWS, d), lambda i: (i, 0)), full((d, 2 * CONV_CH)),
                  full((CONV_WIDTH, CONV_CH)), full((1, CONV_CH)), full((1, CONV_CH)),
                  full((1, CONV_CH))],
        out_specs=pl.BlockSpec((CONV_ROWS, CONV_CH), lambda i: (i, 0)),
        out_shape=jax.ShapeDtypeStruct((t, CONV_CH), BF16),
        scratch_shapes=[pltpu.VMEM((CONV_ROWS + CONV_HALO, CONV_CH), F32),
                        pltpu.VMEM((SUBLANES - 1, CONV_ROWS + CONV_HALO - SUBLANES, CONV_CH), F32)],
        compiler_params=_params("arbitrary"),
        name="conv_module",
    )(h, w_in_conv, conv_w, row(conv_b), row(norm_g), row(norm_b))


def _rwkv_masks(step_rows):
    n = GROUP_W
    r = np.arange(n)[:, None]
    c = np.arange(n)[None, :]
    same = lambda w: (r // w) == (c // w)
    strict = (r > c) & same(CHUNK)
    incl = (r >= c) & same(CHUNK)
    m16 = (r > c) & same(16)
    m32 = (r > c) & same(32) & ~same(16)
    m64 = (r > c) & same(64) & ~same(32)
    stack = (c // RWKV_HEAD) == (r // CHUNK)
    masks = np.stack([strict, incl, m16, m32, m64, stack]).astype(np.float32)
    tr = np.arange(step_rows)
    tri = ((tr[:, None] >= tr[None, :]) & ((tr[:, None] // CHUNK) == (tr[None, :] // CHUNK)))
    tri = tri.astype(np.float32)
    hr = np.arange(RWKV_CH)
    head_ones = ((hr[:, None] // RWKV_HEAD) == (hr[None, :] // RWKV_HEAD)).astype(np.float32)
    return masks, tri, head_ones


def _each(f, *cols):
    return [f(*xs) for xs in zip(*cols)]


def _unit_lower_inverse_minus_identity(labs, m16, m32, m64, passes):
    mm = functools.partial(_dot, passes=passes)
    d = [lab * m16 for lab in labs]
    d2 = _each(mm, d, d)
    d4 = _each(mm, d2, d2)
    dd2 = _each(mm, d, d2)
    d8 = _each(mm, d4, d4)
    n = _each(lambda x, x2, xx2: x2 - x - xx2, d, d2, dd2)
    n = _each(lambda x, y, xy: x + y + xy, n, d4, _each(mm, n, d4))
    n = _each(lambda x, y, xy: x + y + xy, n, d8, _each(mm, n, d8))
    for mask in (m32, m64):
        off = [lab * mask for lab in labs]
        w = _each(lambda o, no: o + no, off, _each(mm, n, off))
        w = _each(lambda x, xn: x + xn, w, _each(mm, w, n))
        n = _each(lambda x, y: x - y, n, w)
    return n


def _rwkv_kernel(u_ref, mu_ref, w0_ref, w2_ref, a0_ref, a2_ref, g2_ref, kk_ref, ka_ref, rk_ref,
                 lg_ref, lb_ref, masks_ref, tri_ref, ones_ref, o_ref, s_ref, carry_ref):
    c = CHUNK
    rows = u_ref.shape[0]
    n_groups = RWKV_CH // GROUP_W
    w = GROUP_W

    @pl.when(pl.program_id(1) == 0)
    def _():
        s_ref[...] = jnp.zeros_like(s_ref)
        carry_ref[...] = jnp.zeros_like(carry_ref)

    z = u_ref[...]
    prev = pltpu.roll(z, 1, 0)
    row_id = lax.broadcasted_iota(jnp.int32, z.shape, 0)
    prev = jnp.where(row_id == 0, jnp.broadcast_to(carry_ref[7:8, :], z.shape), prev)
    carry_ref[...] = z[rows - 8:rows, :]
    z = z + (prev - z) * mu_ref[...]

    r = z[:, 0:RWKV_CH]
    k = z[:, RWKV_CH:2 * RWKV_CH]
    v = z[:, 2 * RWKV_CH:3 * RWKV_CH]
    wa_lo = z[:, 3 * RWKV_CH:3 * RWKV_CH + 128]
    g_lo = z[:, 3 * RWKV_CH + 128:3 * RWKV_CH + 256]

    wpre = w0_ref[...] + _dot(jnp.tanh(wa_lo), w2_ref[...], passes=3)
    ew = _sigmoid(wpre) * math.exp(-0.5)
    a = _sigmoid(a0_ref[...] + _dot(wa_lo, a2_ref[...], passes=3))
    gate = _dot(_sigmoid(g_lo), g2_ref[...], passes=3)

    ones = ones_ref[...]
    kk = k * kk_ref[...]
    ss = _dot_exact_rhs(kk * kk, ones)
    kk = kk * lax.rsqrt(jnp.maximum(ss, 1e-24))
    k2 = k * (1.0 + (a - 1.0) * ka_ref[...])
    kka = kk * a

    e0 = ew.astype(BF16)
    e1f = ew - e0.astype(F32)
    e1 = e1f.astype(BF16)
    e2 = (e1f - e1.astype(F32)).astype(BF16)
    tri = tri_ref[...]
    cum = (jnp.dot(tri, e0, preferred_element_type=F32) + jnp.dot(tri, e1, preferred_element_type=F32)
           + jnp.dot(tri, e2, preferred_element_type=F32))
    gam = jnp.exp(-cum)
    igam = jnp.exp(cum)
    r_t = r * gam
    ka_t = kk * jnp.exp(ew - cum)
    b_t = kka * igam
    k_t = k2 * igam

    strict, incl, m16, m32, m64, stack = (masks_ref[i] for i in range(6))
    p1 = 1
    mm = functools.partial(_dot, passes=p1)

    units = [(q, g) for q in range(rows // c) for g in range(n_groups)]

    def stacked(x):
        return [jnp.concatenate([x[q * c:(q + 1) * c, g * w:(g + 1) * w]] * HEADS_PER_GROUP, axis=0)
                * stack for q, g in units]

    xa, xr, xv, yb, yk = stacked(ka_t), stacked(r_t), stacked(v), stacked(b_t), stacked(k_t)
    xar = _each(lambda top, bot: jnp.concatenate([top, bot], axis=0), xa, xr)
    pb = _each(lambda x, y: mm(x, y, NT), xar, yb)
    pk = _each(lambda x, y: mm(x, y, NT), xar, yk)
    lab = [p[:w] * strict for p in pb]
    arb = [p[w:] * incl for p in pb]
    lak = [p[:w] * strict for p in pk]
    ark = [p[w:] * incl for p in pk]
    n_inv = _unit_lower_inverse_minus_identity(lab, m16, m32, m64, p1)
    lv = _each(mm, lak, xv)
    rhs = _each(lambda left, right: jnp.concatenate([left, right], axis=1), xa, lv)
    gz = _each(lambda x, nx: x + nx, rhs, _each(mm, n_inv, rhs))
    corr = _each(mm, arb, gz)
    qh = _each(lambda x, cr: x - cr[:, :w], xr, corr)
    yloc = _each(lambda av, cr: av - cr[:, w:], _each(mm, ark, xv), corr)
    gzb = _each(lambda x, y: mm(x, y, TN), gz, yb)
    vk = _each(lambda x, y: mm(x, y, TN), xv, yk)

    state = [s_ref[g] for g in range(n_groups)]
    y_rows = []
    for q in range(rows // c):
        gam_end = gam[(q + 1) * c - 1:(q + 1) * c, :]
        ys = []
        for g in range(n_groups):
            i = q * n_groups + g
            s0 = state[g]
            y_st = mm(qh[i], s0, NT) + yloc[i]
            y_heads = y_st[0:c]
            for h in range(1, HEADS_PER_GROUP):
                y_heads = y_heads + y_st[h * c:(h + 1) * c]
            ys.append(y_heads)
            s_new = s0 - mm(s0, gzb[i][:w]) + vk[i] - gzb[i][w:]
            state[g] = s_new * gam_end[:, g * w:(g + 1) * w]
        y_rows.append(jnp.concatenate(ys, axis=1))
    for g in range(n_groups):
        s_ref[g] = state[g]
    y = jnp.concatenate(y_rows, axis=0)

    inv_n = 1.0 / RWKV_HEAD
    mean = _dot_exact_rhs(y, ones) * inv_n
    yc = y - mean
    var = _dot_exact_rhs(yc * yc, ones) * inv_n
    yn = yc * lax.rsqrt(var + LNX_EPS) * lg_ref[...] + lb_ref[...]
    bonus = _dot_exact_rhs(r * k2 * rk_ref[...], ones) * v
    o_ref[...] = ((yn + bonus) * gate).astype(o_ref.dtype)


def _rwkv_time_mix(u_rwkv, n_batch, shift_mu, w0, w2, a0, a2, g2, k_k, k_a, r_k, lnx_g, lnx_b):
    t = u_rwkv.shape[0]
    step_rows = CHUNK * RWKV_CHUNKS_PER_STEP
    n_steps = t // n_batch // step_rows
    masks, tri, head_ones = _rwkv_masks(step_rows)
    row = lambda v: v.reshape(1, -1).astype(F32)
    zeros64 = jnp.zeros((64, RWKV_CH), F32)
    w2p = jnp.concatenate([w2, zeros64], axis=0)
    a2p = jnp.concatenate([zeros64, a2], axis=0)
    full2 = lambda shape: pl.BlockSpec(shape, lambda b, c: (0, 0))
    vec = full2((1, RWKV_CH))
    return pl.pallas_call(
        _rwkv_kernel,
        grid=(n_batch, n_steps),
        in_specs=[pl.BlockSpec((step_rows, RWKV_COLS), lambda b, c: (b * n_steps + c, 0)),
                  full2((1, RWKV_COLS)), vec, full2((128, RWKV_CH)), vec, full2((128, RWKV_CH)),
                  full2((128, RWKV_CH)), vec, vec, vec, vec, vec,
                  pl.BlockSpec(masks.shape, lambda b, c: (0, 0, 0)),
                  full2((step_rows, step_rows)), full2((RWKV_CH, RWKV_CH))],
        out_specs=pl.BlockSpec((step_rows, RWKV_CH), lambda b, c: (b * n_steps + c, 0)),
        out_shape=jax.ShapeDtypeStruct((t, RWKV_CH), BF16),
        scratch_shapes=[pltpu.VMEM((RWKV_CH // GROUP_W, GROUP_W, GROUP_W), F32),
                        pltpu.VMEM((8, RWKV_COLS), F32)],
        compiler_params=_params("arbitrary", "arbitrary"),
        name="rwkv7_time_mix",
    )(u_rwkv, row(shift_mu), row(w0), w2p, row(a0), a2p, g2, row(k_k), row(k_a), row(r_k),
      row(lnx_g), row(lnx_b), jnp.asarray(masks), jnp.asarray(tri, BF16),
      jnp.asarray(head_ones, BF16))


def _proj_ln_kernel(n_in, has_bias, *refs):
    h_ref = refs[0]
    xs = refs[1:1 + n_in]
    ws = refs[1 + n_in:1 + 2 * n_in]
    rest = refs[1 + 2 * n_in:]
    if has_bias:
        bias_ref, g_ref, b_ref, o_ref = rest
    else:
        g_ref, b_ref, o_ref = rest
    acc = ALPHA * h_ref[...]
    for x_ref, w_ref in zip(xs, ws):
        acc = acc + jnp.dot(x_ref[...].astype(BF16), w_ref[...], preferred_element_type=F32)
    if has_bias:
        acc = acc + bias_ref[...]
    o_ref[...] = _layer_norm(acc, g_ref[...], b_ref[...], LN_EPS)


def _proj_ln(h, xs, ws, bias, g, b, tm):
    t, d = h.shape
    row = lambda v: v.reshape(1, -1)
    full = lambda shape: pl.BlockSpec(shape, lambda i: (0, 0))
    in_specs = [pl.BlockSpec((tm, d), lambda i: (i, 0))]
    in_specs += [pl.BlockSpec((tm, x.shape[1]), lambda i: (i, 0)) for x in xs]
    in_specs += [full(w.shape) for w in ws]
    args = [h, *xs, *ws]
    if bias is not None:
        in_specs.append(full((1, d)))
        args.append(row(bias))
    in_specs += [full((1, d)), full((1, d))]
    args += [row(g), row(b)]
    return pl.pallas_call(
        functools.partial(_proj_ln_kernel, len(xs), bias is not None),
        grid=(t // tm,),
        in_specs=in_specs,
        out_specs=pl.BlockSpec((tm, d), lambda i: (i, 0)),
        out_shape=jax.ShapeDtypeStruct((t, d), F32),
        compiler_params=_params("parallel"),
        name="proj_residual_ln",
    )(*args)


SWIGLU_ROWS = 256


def _swiglu_rows(x_ref, wg_ref, wu_ref, wd_ref, rows):
    xb = x_ref[rows, :].astype(BF16)
    gate = jnp.dot(xb, wg_ref[...], preferred_element_type=F32)
    up = jnp.dot(xb, wu_ref[...], preferred_element_type=F32)
    act = (gate * _sigmoid(gate) * up).astype(BF16)
    return jnp.dot(act, wd_ref[...], preferred_element_type=F32)


def _row_blocks(n_rows):
    assert n_rows % SWIGLU_ROWS == 0
    return [slice(r0, r0 + SWIGLU_ROWS) for r0 in range(0, n_rows, SWIGLU_ROWS)]


def _ffn_ln_kernel(h_ref, wg_ref, wu_ref, wd_ref, g_ref, b_ref, o_ref):
    for rows in _row_blocks(h_ref.shape[0]):
        acc = ALPHA * h_ref[rows, :] + _swiglu_rows(h_ref, wg_ref, wu_ref, wd_ref, rows)
        o_ref[rows, :] = _layer_norm(acc, g_ref[...], b_ref[...], LN_EPS)


def _ffn_ln(h, wg, wu, wd, g, b, tm):
    t, d = h.shape
    dff = wg.shape[1]
    row = lambda v: v.reshape(1, -1)
    full = lambda shape: pl.BlockSpec(shape, lambda i: (0, 0))
    return pl.pallas_call(
        _ffn_ln_kernel,
        grid=(t // tm,),
        in_specs=[pl.BlockSpec((tm, d), lambda i: (i, 0)), full((d, dff)), full((d, dff)),
                  full((dff, d)), full((1, d)), full((1, d))],
        out_specs=pl.BlockSpec((tm, d), lambda i: (i, 0)),
        out_shape=jax.ShapeDtypeStruct((t, d), F32),
        compiler_params=_params("parallel"),
        name="swiglu_residual_ln",
    )(h, wg, wu, wd, row(g), row(b))


def _cast_kernel(*refs):
    n = len(refs) // 2
    for src, dst in zip(refs[:n], refs[n:]):
        dst[...] = src[...].astype(dst.dtype)


CAST_STEPS = 32


def _cast_expert_weights(ws):
    flat = [w.reshape(-1, w.shape[-1]) for w in ws]
    assert all(f.shape[0] % (8 * CAST_STEPS) == 0 for f in flat)
    specs = [pl.BlockSpec((f.shape[0] // CAST_STEPS, f.shape[1]), lambda i: (i, 0)) for f in flat]
    outs = pl.pallas_call(
        _cast_kernel,
        grid=(CAST_STEPS,),
        in_specs=specs,
        out_specs=specs,
        out_shape=[jax.ShapeDtypeStruct(f.shape, BF16) for f in flat],
        compiler_params=_params("parallel"),
        name="cast_bf16",
    )(*flat)
    return [o.reshape(w.shape) for o, w in zip(outs, ws)]


Q_COLS = N_Q_HEADS * HEAD_DIM
K_COLS = 2 * N_KV_HEADS * 128
V_COLS = N_KV_HEADS * 128
KV_COLS = K_COLS + V_COLS


def _qkv_rope_kernel(h_ref, w_ref, bias_ref, cos_ref, sin_ref, q_ref, kv_ref):
    acc = jnp.dot(h_ref[...].astype(BF16), w_ref[...], preferred_element_type=F32) + bias_ref[...]
    cos = cos_ref[...]
    sin = sin_ref[...]
    lane = lax.broadcasted_iota(jnp.int32, cos.shape, 1)
    first_half = (lane % HEAD_DIM) < (HEAD_DIM // 2)
    for c0 in range(0, Q_COLS + K_COLS, 128):
        x = acc[:, c0:c0 + 128]
        swapped = jnp.where(first_half, pltpu.roll(x, 128 - HEAD_DIM // 2, 1),
                            pltpu.roll(x, HEAD_DIM // 2, 1))
        y = (x * cos + swapped * sin).astype(BF16)
        if c0 < Q_COLS:
            q_ref[:, c0:c0 + 128] = y
        else:
            kv_ref[:, c0 - Q_COLS:c0 - Q_COLS + 128] = y
    kv_ref[:, K_COLS:] = acc[:, Q_COLS + K_COLS:].astype(BF16)


def _rope_tables(lp):
    half = HEAD_DIM // 2
    inv = ROPE_THETA ** (-jnp.arange(half, dtype=F32) / half)
    pos = (jnp.arange(lp) - FRONT_PAD).astype(F32)
    ang = pos[:, None] * inv[None, :]
    cos = jnp.cos(ang)
    sin = jnp.sin(ang)
    cos = jnp.concatenate([cos, cos] * (128 // HEAD_DIM), axis=1)
    sin = jnp.concatenate([-sin, sin] * (128 // HEAD_DIM), axis=1)
    return cos, sin


def _qkv_rope(h, w, bias, lp, tm):
    t, d = h.shape
    n = w.shape[1]
    cos, sin = _rope_tables(lp)
    per_seq = lp // tm
    return pl.pallas_call(
        _qkv_rope_kernel,
        grid=(t // tm,),
        in_specs=[pl.BlockSpec((tm, d), lambda i: (i, 0)),
                  pl.BlockSpec((d, n), lambda i: (0, 0)),
                  pl.BlockSpec((1, n), lambda i: (0, 0)),
                  pl.BlockSpec((tm, 128), lambda i: (i % per_seq, 0)),
                  pl.BlockSpec((tm, 128), lambda i: (i % per_seq, 0))],
        out_specs=[pl.BlockSpec((tm, Q_COLS), lambda i: (i, 0)),
                   pl.BlockSpec((tm, KV_COLS), lambda i: (i, 0))],
        out_shape=[jax.ShapeDtypeStruct((t, Q_COLS), BF16), jax.ShapeDtypeStruct((t, KV_COLS), BF16)],
        compiler_params=_params("parallel"),
        name="qkv_rope",
    )(h, w, bias.reshape(1, -1), cos, sin)


def _qkv_weights(w_qkv, b_qkv):
    kv_w = N_KV_HEADS * HEAD_DIM
    wq, wk, wv = jnp.split(w_qkv, [Q_COLS, Q_COLS + kv_w], axis=1)
    bq, bk, bv = jnp.split(b_qkv[None], [Q_COLS, Q_COLS + kv_w], axis=1)

    def expand(m):
        zero = jnp.zeros((m.shape[0], HEAD_DIM), m.dtype)
        k_cols, v_cols = [], []
        for g in range(N_KV_HEADS):
            col = m[:, g * HEAD_DIM:(g + 1) * HEAD_DIM]
            k_cols += [col, zero, zero, col]
            v_cols += [col, col]
        return k_cols, v_cols

    w_cols = [wq] + expand(wk)[0] + expand(wv)[1]
    b_cols = [bq] + expand(bk)[0] + expand(bv)[1]
    return jnp.concatenate(w_cols, axis=1), jnp.concatenate(b_cols, axis=1)[0]


NEG = -1e30


def _attn_kernel(sink_ref, q_ref, kv_prev_ref, kv_cur_ref, o_ref):
    n = pl.program_id(1)
    blk = ATT_BLOCK
    kv = jnp.concatenate([kv_prev_ref[...], kv_cur_ref[...]], axis=0)
    qi = lax.broadcasted_iota(jnp.int32, (blk, 2 * blk), 0)
    kj = lax.broadcasted_iota(jnp.int32, (blk, 2 * blk), 1)
    first_key = FRONT_PAD - (n - 1) * blk
    valid = (kj > qi) & (kj <= qi + blk) & (kj >= first_key)
    low = lax.broadcasted_iota(jnp.int32, (blk, 128), 1) < HEAD_DIM
    for pair in range(N_Q_HEADS // 2):
        grp = (2 * pair) // GROUP
        vd = kv[:, K_COLS + grp * 128:K_COLS + (grp + 1) * 128]
        qp = q_ref[:, pair * 128:(pair + 1) * 128]
        outs = []
        for half in range(2):
            head = 2 * pair + half
            kh = kv[:, (2 * grp + half) * 128:(2 * grp + half + 1) * 128]
            s = lax.dot_general(qp, kh, NT, preferred_element_type=F32) * (HEAD_DIM ** -0.5)
            s = jnp.where(valid, s, NEG)
            sink = sink_ref[head]
            m = jnp.maximum(jnp.max(s, axis=-1, keepdims=True), sink)
            p = jnp.exp(s - m)
            denom = jnp.sum(p, axis=-1, keepdims=True) + jnp.exp(sink - m)
            o = jnp.dot(p.astype(BF16), vd, preferred_element_type=F32)
            outs.append(o * (1.0 / denom))
        o_ref[:, pair * 128:(pair + 1) * 128] = jnp.where(low, outs[0], outs[1]).astype(o_ref.dtype)


def _attention(q, kv, sinks, n_batch, lp):
    t = q.shape[0]
    nb = lp // ATT_BLOCK
    return pl.pallas_call(
        _attn_kernel,
        grid=(n_batch, nb),
        in_specs=[pl.BlockSpec(memory_space=pltpu.SMEM),
                  pl.BlockSpec((ATT_BLOCK, Q_COLS), lambda b, n: (b * nb + n, 0)),
                  pl.BlockSpec((ATT_BLOCK, KV_COLS), lambda b, n: (b * nb + jnp.maximum(n - 1, 0), 0)),
                  pl.BlockSpec((ATT_BLOCK, KV_COLS), lambda b, n: (b * nb + n, 0))],
        out_specs=pl.BlockSpec((ATT_BLOCK, Q_COLS), lambda b, n: (b * nb + n, 0)),
        out_shape=jax.ShapeDtypeStruct((t, Q_COLS), BF16),
        compiler_params=_params("parallel", "parallel"),
        name="swa_sink_attention",
    )(sinks.astype(F32), q, kv, kv)


LANE_E0, LANE_E1, LANE_W0, LANE_W1, LANE_R0, LANE_R1 = range(6)


def _oproj_ln_route_kernel(h_ref, x_ref, w_ref, bias_ref, g_ref, b_ref, rw_ref, tri_ref,
                           o_ref, ob_ref, route_ref, count_ref, run_ref):
    @pl.when(pl.program_id(0) == 0)
    def _():
        run_ref[...] = jnp.zeros_like(run_ref)

    acc = ALPHA * h_ref[...] + bias_ref[...] + jnp.dot(
        x_ref[...], w_ref[...].astype(BF16), preferred_element_type=F32)
    hn = _layer_norm(acc, g_ref[...], b_ref[...], LN_EPS)
    o_ref[...] = hn
    ob_ref[...] = hn.astype(BF16)

    logits = _dot(hn, rw_ref[...], NN, passes=3)
    lane = lax.broadcasted_iota(jnp.int32, logits.shape, 1)
    logits = jnp.where(lane < N_EXPERTS, logits, -jnp.inf)
    v1 = jnp.max(logits, axis=-1, keepdims=True)
    i1 = jnp.min(jnp.where(logits == v1, lane, 128), axis=-1, keepdims=True)
    rest = jnp.where(lane == i1, -jnp.inf, logits)
    v2 = jnp.max(rest, axis=-1, keepdims=True)
    i2 = jnp.min(jnp.where(rest == v2, lane, 128), axis=-1, keepdims=True)
    w1 = 1.0 / (1.0 + jnp.exp(v2 - v1))
    w2 = 1.0 - w1

    oh1 = (lane == i1).astype(F32)
    oh2 = (lane == i2).astype(F32)
    both = oh1 + oh2
    before = jnp.dot(tri_ref[...], both.astype(BF16), preferred_element_type=F32) + run_ref[0:1, :]
    r1 = jnp.sum(oh1 * before, axis=-1, keepdims=True)
    r2 = jnp.sum(oh2 * before, axis=-1, keepdims=True)
    run_ref[...] = run_ref[...] + jnp.sum(both, axis=0, keepdims=True)
    count_ref[...] = run_ref[...]

    out = jnp.where(lane == LANE_E0, i1.astype(F32), 0.0)
    out = jnp.where(lane == LANE_E1, i2.astype(F32), out)
    out = jnp.where(lane == LANE_W0, w1, out)
    out = jnp.where(lane == LANE_W1, w2, out)
    out = jnp.where(lane == LANE_R0, r1, out)
    out = jnp.where(lane == LANE_R1, r2, out)
    route_ref[...] = out


def _oproj_ln_route(h, x, w, bias, g, b, router_w, tm):
    t, d = h.shape
    row = lambda v: v.reshape(1, -1)
    rw = jnp.zeros((d, 128), F32).at[:, :N_EXPERTS].set(router_w)
    tri = jnp.asarray(np.tril(np.ones((tm, tm), np.float32), -1), BF16)
    tile = lambda width: pl.BlockSpec((tm, width), lambda i: (i, 0))
    full = lambda shape: pl.BlockSpec(shape, lambda i: (0, 0))
    return pl.pallas_call(
        _oproj_ln_route_kernel,
        grid=(t // tm,),
        in_specs=[tile(d), tile(x.shape[1]), full(w.shape), full((1, d)), full((1, d)), full((1, d)),
                  full((d, 128)), full((tm, tm))],
        out_specs=[tile(d), tile(d), tile(128), full((8, 128))],
        out_shape=[jax.ShapeDtypeStruct((t, d), F32), jax.ShapeDtypeStruct((t, d), BF16),
                   jax.ShapeDtypeStruct((t, 128), F32), jax.ShapeDtypeStruct((8, 128), F32)],
        scratch_shapes=[pltpu.VMEM((8, 128), F32)],
        compiler_params=_params("arbitrary"),
        name="oproj_ln_route",
    )(h, x, w, row(bias), row(g), row(b), rw, tri)


def _moe_kernel(n_ff, te_ref, tv_ref, x_ref, wg_ref, wu_ref, wd_ref, o_ref, acc_ref):
    i = pl.program_id(0)
    j = pl.program_id(1)
    last = n_ff - 1

    def step(first, final):
        for rows in _row_blocks(x_ref.shape[0]):
            part = _swiglu_rows(x_ref, wg_ref.at[0], wu_ref.at[0], wd_ref.at[0], rows)
            if not first:
                part = acc_ref[rows, :] + part
            if final:
                o_ref[rows, :] = part.astype(o_ref.dtype)
            else:
                acc_ref[rows, :] = part

    valid = tv_ref[i] > 0
    pl.when(valid & (j == 0))(functools.partial(step, True, False))
    if n_ff > 2:
        pl.when(valid & (j > 0) & (j < last))(functools.partial(step, False, False))
    pl.when(valid & (j == last))(functools.partial(step, False, True))


def _moe_experts(x_sorted, tile_expert, tile_valid, wg, wu, wd, tm, tf):
    n_rows, d = x_sorted.shape
    dff = wg.shape[2]
    grid_spec = pltpu.PrefetchScalarGridSpec(
        num_scalar_prefetch=2,
        grid=(n_rows // tm, dff // tf),
        in_specs=[pl.BlockSpec((tm, d), lambda i, j, te, tv: (i, 0)),
                  pl.BlockSpec((1, d, tf), lambda i, j, te, tv: (te[i], 0, j)),
                  pl.BlockSpec((1, d, tf), lambda i, j, te, tv: (te[i], 0, j)),
                  pl.BlockSpec((1, tf, d), lambda i, j, te, tv: (te[i], j, 0))],
        out_specs=pl.BlockSpec((tm, d), lambda i, j, te, tv: (i, 0)),
        scratch_shapes=[pltpu.VMEM((tm, d), F32)],
    )
    assert dff // tf >= 2
    return pl.pallas_call(
        functools.partial(_moe_kernel, dff // tf),
        grid_spec=grid_spec,
        out_shape=jax.ShapeDtypeStruct((n_rows, d), BF16),
        compiler_params=_params("parallel", "arbitrary"),
        name="moe_expert_swiglu",
    )(tile_expert, tile_valid, x_sorted, wg, wu, wd)


def _route_plan(route, counts, tm):
    top_idx = route[:, LANE_E0:LANE_E1 + 1].astype(jnp.int32)
    rank = route[:, LANE_R0:LANE_R1 + 1].astype(jnp.int32)
    n_assign = top_idx.size
    n_rows = n_assign + N_EXPERTS * tm
    counts = counts[0, :N_EXPERTS].astype(jnp.int32)
    padded = ((counts + tm - 1) // tm) * tm
    ends = jnp.cumsum(padded)
    starts = ends - padded
    pos = starts.at[top_idx].get(mode="promise_in_bounds") + rank
    token = jnp.arange(n_assign, dtype=jnp.int32) // 2
    src = jnp.arange(n_rows, dtype=jnp.int32) % (n_assign // 2)
    src = src.at[pos.reshape(-1)].set(token, unique_indices=True, mode="promise_in_bounds")
    tile_start = jnp.arange(n_rows // tm, dtype=jnp.int32) * tm
    tile_expert = jnp.sum((tile_start[:, None] >= ends[None, :]).astype(jnp.int32), axis=1)
    tile_expert = jnp.minimum(tile_expert, N_EXPERTS - 1)
    tile_valid = (tile_start < ends[-1]).astype(jnp.int32)
    return pos, src, tile_expert, tile_valid


def _combine_ln_kernel(h_ref, y1_ref, y2_ref, r_ref, g_ref, b_ref, o_ref):
    route = r_ref[...]
    acc = (ALPHA * h_ref[...] + route[:, LANE_W0:LANE_W0 + 1] * y1_ref[...].astype(F32)
           + route[:, LANE_W1:LANE_W1 + 1] * y2_ref[...].astype(F32))
    o_ref[...] = _layer_norm(acc, g_ref[...], b_ref[...], LN_EPS)


def _combine_ln(h, y1, y2, route, g, b, tm):
    t, d = h.shape
    row = lambda v: v.reshape(1, -1)
    tile = pl.BlockSpec((tm, d), lambda i: (i, 0))
    vec = pl.BlockSpec((1, d), lambda i: (0, 0))
    return pl.pallas_call(
        _combine_ln_kernel,
        grid=(t // tm,),
        in_specs=[tile, tile, tile, pl.BlockSpec((tm, 128), lambda i: (i, 0)), vec, vec],
        out_specs=tile,
        out_shape=jax.ShapeDtypeStruct((t, d), F32),
        compiler_params=_params("parallel"),
        name="moe_combine_ln",
    )(h, y1, y2, route, row(g), row(b))


def _pick_tile(n, target, mult=16):
    best = mult
    for cand in range(mult, min(n, target) + 1, mult):
        if n % cand == 0:
            best = cand
    return best


def kernel(x, meta_tokens, ev_w_in, ev_conv_w, ev_conv_b, ev_convnorm_g, ev_convnorm_b, ev_shift_mu, ev_w0, ev_w2, ev_a0, ev_a2, ev_g2, ev_k_k, ev_k_a, ev_r_k, ev_lnx_g, ev_lnx_b, ev_w_out, ev_ln1_g, ev_ln1_b, ev_ffn_gate, ev_ffn_up, ev_ffn_down, ev_ln2_g, ev_ln2_b, od_w_qkv, od_b_qkv, od_sinks, od_w_o, od_b_o, od_ln1_g, od_ln1_b, od_router, od_exp_gate, od_exp_up, od_exp_down, od_ln2_g, od_ln2_b):
    n_batch, seq, d = x.shape
    lp = FRONT_PAD + N_META + seq
    assert d == D_MODEL and lp % ATT_BLOCK == 0 and lp % (CHUNK * RWKV_CHUNKS_PER_STEP) == 0
    t = n_batch * lp
    tm = _pick_tile(t, 1024)
    tm_seq = _pick_tile(lp, 1056)
    assert t % CONV_ROWS == 0

    meta = jnp.broadcast_to(meta_tokens[None].astype(x.dtype), (n_batch, N_META, d))
    h = jnp.concatenate([jnp.zeros((n_batch, FRONT_PAD, d), x.dtype), meta, x], axis=1)
    h = h.reshape(t, d)

    w_in = ev_w_in[0].astype(BF16)
    u_rwkv = _matmul(h, w_in[:, 2 * CONV_CH:], tm, RWKV_COLS // 2, F32)
    a_out = _conv_module(h, w_in[:, :2 * CONV_CH], ev_conv_w[0], ev_conv_b[0], ev_convnorm_g[0],
                         ev_convnorm_b[0])
    b_out = _rwkv_time_mix(u_rwkv, n_batch, ev_shift_mu[0], ev_w0[0], ev_w2[0], ev_a0[0], ev_a2[0],
                           ev_g2[0], ev_k_k[0], ev_k_a[0], ev_r_k[0], ev_lnx_g[0], ev_lnx_b[0])
    w_out = ev_w_out[0].astype(BF16)
    h = _proj_ln(h, [a_out, b_out], [w_out[:CONV_CH], w_out[CONV_CH:]], None,
                 ev_ln1_g[0], ev_ln1_b[0], tm)
    h = _ffn_ln(h, ev_ffn_gate[0].astype(BF16), ev_ffn_up[0].astype(BF16),
                ev_ffn_down[0].astype(BF16), ev_ln2_g[0], ev_ln2_b[0],
                _pick_tile(t, 512, SWIGLU_ROWS))

    w_qkv, b_qkv = _qkv_weights(od_w_qkv[0], od_b_qkv[0])
    q, kv = _qkv_rope(h, w_qkv.astype(BF16), b_qkv, lp, tm_seq)
    att = _attention(q, kv, od_sinks[0], n_batch, lp)
    h, h_bf16, route, counts = _oproj_ln_route(h, att, od_w_o[0], od_b_o[0], od_ln1_g[0], od_ln1_b[0],
                                               od_router[0], tm)
    tm_moe = 1024 if (2 * t) % 1024 == 0 else _pick_tile(2 * t, 1024)
    pos, src, tile_expert, tile_valid = _route_plan(route, counts, tm_moe)
    x_sorted = h_bf16.at[src].get(mode="promise_in_bounds")
    exp_w = _cast_expert_weights([od_exp_gate[0], od_exp_up[0], od_exp_down[0]])
    y_sorted = _moe_experts(x_sorted, tile_expert, tile_valid, *exp_w, tm_moe, 1792)
    y1 = y_sorted.at[pos[:, 0]].get(mode="promise_in_bounds", unique_indices=True)
    y2 = y_sorted.at[pos[:, 1]].get(mode="promise_in_bounds", unique_indices=True)
    h = _combine_ln(h, y1, y2, route, od_ln2_g[0], od_ln2_b[0], tm)

    return h.reshape(n_batch, lp, d)[:, FRONT_PAD + N_META:]
```

```python
import functools
import math

import jax
import jax.numpy as jnp
import numpy as np
from jax import lax
from jax.experimental import pallas as pl
from jax.experimental.pallas import tpu as pltpu

F32 = jnp.float32
BF16 = jnp.bfloat16

D_MODEL = 1024
N_META = 16
ATT_BLOCK = 128
FRONT_PAD = (-N_META) % ATT_BLOCK
CONV_CH = 512
CONV_WIDTH = 31
RWKV_CH = 512
RWKV_HEAD = 64
RWKV_COLS = 3 * RWKV_CH + 64 + 64 + 128
HEAD_DIM = 64
N_Q_HEADS = 16
N_KV_HEADS = 2
GROUP = N_Q_HEADS // N_KV_HEADS
ROPE_THETA = 10000.0
N_EXPERTS = 8
DEPTH = 2
ALPHA = (2 * DEPTH) ** 0.25
LN_EPS = 1e-5
LNX_EPS = 64e-5

CHUNK = 64
RWKV_CHUNKS_PER_STEP = 3
HEADS_PER_GROUP = 2
GROUP_W = HEADS_PER_GROUP * RWKV_HEAD
VMEM_LIMIT = 56 * 1024 * 1024

NN = (((1,), (0,)), ((), ()))
NT = (((1,), (1,)), ((), ()))
TN = (((0,), (0,)), ((), ()))


def _params(*sem):
    return pltpu.CompilerParams(dimension_semantics=sem, vmem_limit_bytes=VMEM_LIMIT)


def _sigmoid(x):
    return 1.0 / (1.0 + jnp.exp(-x))


def _split2(x):
    hi = x.astype(BF16)
    lo = (x - hi.astype(F32)).astype(BF16)
    return hi, lo


def _dot(a, b, dims=NN, passes=1):
    if passes == 1:
        return lax.dot_general(a.astype(BF16), b.astype(BF16), dims, preferred_element_type=F32)
    ah, al = _split2(a)
    bh, bl = _split2(b)
    out = lax.dot_general(ah, bh, dims, preferred_element_type=F32)
    out = out + lax.dot_general(ah, bl, dims, preferred_element_type=F32)
    return out + lax.dot_general(al, bh, dims, preferred_element_type=F32)


def _dot_exact_rhs(a, b_bf16, dims=NN, parts=2):
    out = None
    rem = a
    for _ in range(parts):
        piece = rem.astype(BF16)
        term = lax.dot_general(piece, b_bf16, dims, preferred_element_type=F32)
        out = term if out is None else out + term
        rem = rem - piece.astype(F32)
    return out


def _layer_norm(x, g, b, eps):
    mu = jnp.mean(x, axis=-1, keepdims=True)
    xc = x - mu
    var = jnp.mean(xc * xc, axis=-1, keepdims=True)
    return xc * lax.rsqrt(var + eps) * g + b


CONV_ROWS = 256
CONV_SUB = 32
CONV_HALO = 32
SUBLANES = 8


def _conv_kernel(h_ref, w_ref, cw_ref, cb_ref, g_ref, b_ref, o_ref, buf_ref, sh_ref):
    @pl.when(pl.program_id(0) == 0)
    def _():
        buf_ref[0:CONV_HALO, :] = jnp.zeros((CONV_HALO, CONV_CH), F32)

    u = jnp.dot(h_ref[...].astype(BF16), w_ref[...], preferred_element_type=F32)
    buf_ref[CONV_HALO:CONV_HALO + CONV_ROWS, :] = u[:, :CONV_CH] * _sigmoid(u[:, CONV_CH:])
    n_sh = CONV_HALO + CONV_ROWS - SUBLANES
    for r in range(1, SUBLANES):
        sh_ref[r - 1] = buf_ref[r:r + n_sh, :]
    cw = cw_ref[...]
    base = CONV_HALO - (CONV_WIDTH - 1)
    for s in range(CONV_ROWS // CONV_SUB):
        r0 = s * CONV_SUB
        acc = jnp.broadcast_to(cb_ref[...], (CONV_SUB, CONV_CH))
        for j in range(CONV_WIDTH):
            shift = (base + j) % SUBLANES
            start = base + j - shift + r0
            if shift == 0:
                win = buf_ref[start:start + CONV_SUB, :]
            else:
                win = sh_ref[shift - 1, start:start + CONV_SUB, :]
            acc = acc + cw[j:j + 1, :] * win
        y = _layer_norm(acc, g_ref[...], b_ref[...], LN_EPS)
        o_ref[r0:r0 + CONV_SUB, :] = (y * _sigmoid(y)).astype(o_ref.dtype)
    buf_ref[0:CONV_HALO, :] = buf_ref[CONV_ROWS:CONV_ROWS + CONV_HALO, :]


def _conv_module(h, w_in_conv, conv_w, conv_b, norm_g, norm_b):
    t, d = h.shape
    row = lambda v: v.reshape(1, -1)
    full = lambda shape: pl.BlockSpec(shape, lambda i: (0, 0))
    return pl.pallas_call(
        _conv_kernel,
        grid=(t // CONV_ROWS,),
        in_specs=[pl.BlockSpec((CONV_ROWS, d), lambda i: (i, 0)), full((d, 2 * CONV_CH)),
                  full((CONV_WIDTH, CONV_CH)), full((1, CONV_CH)), full((1, CONV_CH)),
                  full((1, CONV_CH))],
        out_specs=pl.BlockSpec((CONV_ROWS, CONV_CH), lambda i: (i, 0)),
        out_shape=jax.ShapeDtypeStruct((t, CONV_CH), BF16),
        scratch_shapes=[pltpu.VMEM((CONV_ROWS + CONV_HALO, CONV_CH), F32),
                        pltpu.VMEM((SUBLANES - 1, CONV_ROWS + CONV_HALO - SUBLANES, CONV_CH), F32)],
        compiler_params=_params("arbitrary"),
        name="conv_module",
    )(h, w_in_conv, conv_w, row(conv_b), row(norm_g), row(norm_b))


def _rwkv_masks(step_rows):
    n = GROUP_W
    r = np.arange(n)[:, None]
    c = np.arange(n)[None, :]
    same = lambda w: (r // w) == (c // w)
    strict = (r > c) & same(CHUNK)
    incl = (r >= c) & same(CHUNK)
    m16 = (r > c) & same(16)
    m32 = (r > c) & same(32) & ~same(16)
    m64 = (r > c) & same(64) & ~same(32)
    stack = (c // RWKV_HEAD) == (r // CHUNK)
    masks = np.stack([strict, incl, m16, m32, m64, stack]).astype(np.float32)
    tr = np.arange(step_rows)
    tri = ((tr[:, None] >= tr[None, :]) & ((tr[:, None] // CHUNK) == (tr[None, :] // CHUNK)))
    tri = tri.astype(np.float32)
    hr = np.arange(RWKV_CH)
    head_ones = ((hr[:, None] // RWKV_HEAD) == (hr[None, :] // RWKV_HEAD)).astype(np.float32)
    return masks, tri, head_ones


def _each(f, *cols):
    return [f(*xs) for xs in zip(*cols)]


def _unit_lower_inverse_minus_identity(labs, m16, m32, m64, passes):
    mm = functools.partial(_dot, passes=passes)
    d = [lab * m16 for lab in labs]
    d2 = _each(mm, d, d)
    d4 = _each(mm, d2, d2)
    dd2 = _each(mm, d, d2)
    d8 = _each(mm, d4, d4)
    n = _each(lambda x, x2, xx2: x2 - x - xx2, d, d2, dd2)
    n = _each(lambda x, y, xy: x + y + xy, n, d4, _each(mm, n, d4))
    n = _each(lambda x, y, xy: x + y + xy, n, d8, _each(mm, n, d8))
    for mask in (m32, m64):
        off = [lab * mask for lab in labs]
        w = _each(lambda o, no: o + no, off, _each(mm, n, off))
        w = _each(lambda x, xn: x + xn, w, _each(mm, w, n))
        n = _each(lambda x, y: x - y, n, w)
    return n


def _rwkv_kernel(h_ref, win_ref, mu_ref, w0_ref, w2_ref, a0_ref, a2_ref, g2_ref, kk_ref, ka_ref,
                 rk_ref, lg_ref, lb_ref, masks_ref, tri_ref, ones_ref, o_ref, s_ref, carry_ref):
    c = CHUNK
    rows = h_ref.shape[0]
    n_groups = RWKV_CH // GROUP_W
    w = GROUP_W

    @pl.when(pl.program_id(1) == 0)
    def _():
        s_ref[...] = jnp.zeros_like(s_ref)
        carry_ref[...] = jnp.zeros_like(carry_ref)

    z = jnp.dot(h_ref[...].astype(BF16), win_ref[...], preferred_element_type=F32)
    prev = pltpu.roll(z, 1, 0)
    row_id = lax.broadcasted_iota(jnp.int32, z.shape, 0)
    prev = jnp.where(row_id == 0, jnp.broadcast_to(carry_ref[7:8, :], z.shape), prev)
    carry_ref[...] = z[rows - 8:rows, :]
    z = z + (prev - z) * mu_ref[...]

    r = z[:, 0:RWKV_CH]
    k = z[:, RWKV_CH:2 * RWKV_CH]
    v = z[:, 2 * RWKV_CH:3 * RWKV_CH]
    wa_lo = z[:, 3 * RWKV_CH:3 * RWKV_CH + 128]
    g_lo = z[:, 3 * RWKV_CH + 128:3 * RWKV_CH + 256]

    wpre = w0_ref[...] + _dot(jnp.tanh(wa_lo), w2_ref[...], passes=3)
    ew = _sigmoid(wpre) * math.exp(-0.5)
    a = _sigmoid(a0_ref[...] + _dot(wa_lo, a2_ref[...], passes=3))
    gate = _dot(_sigmoid(g_lo), g2_ref[...], passes=3)

    ones = ones_ref[...]
    kk = k * kk_ref[...]
    ss = _dot_exact_rhs(kk * kk, ones)
    kk = kk * lax.rsqrt(jnp.maximum(ss, 1e-24))
    k2 = k * (1.0 + (a - 1.0) * ka_ref[...])
    kka = kk * a

    e0 = ew.astype(BF16)
    e1f = ew - e0.astype(F32)
    e1 = e1f.astype(BF16)
    e2 = (e1f - e1.astype(F32)).astype(BF16)
    tri = tri_ref[...]
    cum = (jnp.dot(tri, e0, preferred_element_type=F32) + jnp.dot(tri, e1, preferred_element_type=F32)
           + jnp.dot(tri, e2, preferred_element_type=F32))
    gam = jnp.exp(-cum)
    igam = jnp.exp(cum)
    r_t = r * gam
    ka_t = kk * jnp.exp(ew - cum)
    b_t = kka * igam
    k_t = k2 * igam

    strict, incl, m16, m32, m64, stack = (masks_ref[i] for i in range(6))
    p1 = 1
    mm = functools.partial(_dot, passes=p1)

    units = [(q, g) for q in range(rows // c) for g in range(n_groups)]

    def stacked(x):
        return [jnp.concatenate([x[q * c:(q + 1) * c, g * w:(g + 1) * w]] * HEADS_PER_GROUP, axis=0)
                * stack for q, g in units]

    xa, xr, xv, yb, yk = stacked(ka_t), stacked(r_t), stacked(v), stacked(b_t), stacked(k_t)
    xar = _each(lambda top, bot: jnp.concatenate([top, bot], axis=0), xa, xr)
    pb = _each(lambda x, y: mm(x, y, NT), xar, yb)
    pk = _each(lambda x, y: mm(x, y, NT), xar, yk)
    lab = [p[:w] * strict for p in pb]
    arb = [p[w:] * incl for p in pb]
    lak = [p[:w] * strict for p in pk]
    ark = [p[w:] * incl for p in pk]
    n_inv = _unit_lower_inverse_minus_identity(lab, m16, m32, m64, p1)
    lv = _each(mm, lak, xv)
    rhs = _each(lambda left, right: jnp.concatenate([left, right], axis=1), xa, lv)
    gz = _each(lambda x, nx: x + nx, rhs, _each(mm, n_inv, rhs))
    corr = _each(mm, arb, gz)
    qh = _each(lambda x, cr: x - cr[:, :w], xr, corr)
    yloc = _each(lambda av, cr: av - cr[:, w:], _each(mm, ark, xv), corr)
    gzb = _each(lambda x, y: mm(x, y, TN), gz, yb)
    vk = _each(lambda x, y: mm(x, y, TN), xv, yk)

    state = [s_ref[g] for g in range(n_groups)]
    y_rows = []
    for q in range(rows // c):
        gam_end = gam[(q + 1) * c - 1:(q + 1) * c, :]
        ys = []
        for g in range(n_groups):
            i = q * n_groups + g
            s0 = state[g]
            y_st = mm(qh[i], s0, NT) + yloc[i]
            y_heads = y_st[0:c]
            for h in range(1, HEADS_PER_GROUP):
                y_heads = y_heads + y_st[h * c:(h + 1) * c]
            ys.append(y_heads)
            s_new = s0 - mm(s0, gzb[i][:w]) + vk[i] - gzb[i][w:]
            state[g] = s_new * gam_end[:, g * w:(g + 1) * w]
        y_rows.append(jnp.concatenate(ys, axis=1))
    for g in range(n_groups):
        s_ref[g] = state[g]
    y = jnp.concatenate(y_rows, axis=0)

    inv_n = 1.0 / RWKV_HEAD
    mean = _dot_exact_rhs(y, ones) * inv_n
    yc = y - mean
    var = _dot_exact_rhs(yc * yc, ones) * inv_n
    yn = yc * lax.rsqrt(var + LNX_EPS) * lg_ref[...] + lb_ref[...]
    bonus = _dot_exact_rhs(r * k2 * rk_ref[...], ones) * v
    o_ref[...] = ((yn + bonus) * gate).astype(o_ref.dtype)


def _rwkv_time_mix(h, w_in_rwkv, n_batch, shift_mu, w0, w2, a0, a2, g2, k_k, k_a, r_k, lnx_g, lnx_b):
    t, d = h.shape
    step_rows = CHUNK * RWKV_CHUNKS_PER_STEP
    n_steps = t // n_batch // step_rows
    masks, tri, head_ones = _rwkv_masks(step_rows)
    row = lambda v: v.reshape(1, -1).astype(F32)
    zeros64 = jnp.zeros((64, RWKV_CH), F32)
    w2p = jnp.concatenate([w2, zeros64], axis=0)
    a2p = jnp.concatenate([zeros64, a2], axis=0)
    full2 = lambda shape: pl.BlockSpec(shape, lambda b, c: (0, 0))
    vec = full2((1, RWKV_CH))
    return pl.pallas_call(
        _rwkv_kernel,
        grid=(n_batch, n_steps),
        in_specs=[pl.BlockSpec((step_rows, d), lambda b, c: (b * n_steps + c, 0)),
                  full2((d, RWKV_COLS)), full2((1, RWKV_COLS)), vec, full2((128, RWKV_CH)), vec, full2((128, RWKV_CH)),
                  full2((128, RWKV_CH)), vec, vec, vec, vec, vec,
                  pl.BlockSpec(masks.shape, lambda b, c: (0, 0, 0)),
                  full2((step_rows, step_rows)), full2((RWKV_CH, RWKV_CH))],
        out_specs=pl.BlockSpec((step_rows, RWKV_CH), lambda b, c: (b * n_steps + c, 0)),
        out_shape=jax.ShapeDtypeStruct((t, RWKV_CH), BF16),
        scratch_shapes=[pltpu.VMEM((RWKV_CH // GROUP_W, GROUP_W, GROUP_W), F32),
                        pltpu.VMEM((8, RWKV_COLS), F32)],
        compiler_params=_params("arbitrary", "arbitrary"),
        name="rwkv7_time_mix",
    )(h, w_in_rwkv, row(shift_mu), row(w0), w2p, row(a0), a2p, g2, row(k_k), row(k_a), row(r_k),
      row(lnx_g), row(lnx_b), jnp.asarray(masks), jnp.asarray(tri, BF16),
      jnp.asarray(head_ones, BF16))


def _proj_ln_kernel(n_in, has_bias, *refs):
    h_ref = refs[0]
    xs = refs[1:1 + n_in]
    ws = refs[1 + n_in:1 + 2 * n_in]
    rest = refs[1 + 2 * n_in:]
    if has_bias:
        bias_ref, g_ref, b_ref, o_ref = rest
    else:
        g_ref, b_ref, o_ref = rest
    acc = ALPHA * h_ref[...]
    for x_ref, w_ref in zip(xs, ws):
        acc = acc + jnp.dot(x_ref[...].astype(BF16), w_ref[...], preferred_element_type=F32)
    if has_bias:
        acc = acc + bias_ref[...]
    o_ref[...] = _layer_norm(acc, g_ref[...], b_ref[...], LN_EPS)


def _proj_ln(h, xs, ws, bias, g, b, tm):
    t, d = h.shape
    row = lambda v: v.reshape(1, -1)
    full = lambda shape: pl.BlockSpec(shape, lambda i: (0, 0))
    in_specs = [pl.BlockSpec((tm, d), lambda i: (i, 0))]
    in_specs += [pl.BlockSpec((tm, x.shape[1]), lambda i: (i, 0)) for x in xs]
    in_specs += [full(w.shape) for w in ws]
    args = [h, *xs, *ws]
    if bias is not None:
        in_specs.append(full((1, d)))
        args.append(row(bias))
    in_specs += [full((1, d)), full((1, d))]
    args += [row(g), row(b)]
    return pl.pallas_call(
        functools.partial(_proj_ln_kernel, len(xs), bias is not None),
        grid=(t // tm,),
        in_specs=in_specs,
        out_specs=pl.BlockSpec((tm, d), lambda i: (i, 0)),
        out_shape=jax.ShapeDtypeStruct((t, d), F32),
        compiler_params=_params("parallel"),
        name="proj_residual_ln",
    )(*args)


SWIGLU_ROWS = 256


def _swiglu_rows(x_ref, wg_ref, wu_ref, wd_ref, rows):
    xb = x_ref[rows, :].astype(BF16)
    gate = jnp.dot(xb, wg_ref[...], preferred_element_type=F32)
    up = jnp.dot(xb, wu_ref[...], preferred_element_type=F32)
    act = (gate * _sigmoid(gate) * up).astype(BF16)
    return jnp.dot(act, wd_ref[...], preferred_element_type=F32)


def _row_blocks(n_rows):
    assert n_rows % SWIGLU_ROWS == 0
    return [slice(r0, r0 + SWIGLU_ROWS) for r0 in range(0, n_rows, SWIGLU_ROWS)]


def _ffn_ln_kernel(h_ref, wg_ref, wu_ref, wd_ref, g_ref, b_ref, o_ref):
    for rows in _row_blocks(h_ref.shape[0]):
        acc = ALPHA * h_ref[rows, :] + _swiglu_rows(h_ref, wg_ref, wu_ref, wd_ref, rows)
        o_ref[rows, :] = _layer_norm(acc, g_ref[...], b_ref[...], LN_EPS)


def _ffn_ln(h, wg, wu, wd, g, b, tm):
    t, d = h.shape
    dff = wg.shape[1]
    row = lambda v: v.reshape(1, -1)
    full = lambda shape: pl.BlockSpec(shape, lambda i: (0, 0))
    return pl.pallas_call(
        _ffn_ln_kernel,
        grid=(t // tm,),
        in_specs=[pl.BlockSpec((tm, d), lambda i: (i, 0)), full((d, dff)), full((d, dff)),
                  full((dff, d)), full((1, d)), full((1, d))],
        out_specs=pl.BlockSpec((tm, d), lambda i: (i, 0)),
        out_shape=jax.ShapeDtypeStruct((t, d), F32),
        compiler_params=_params("parallel"),
        name="swiglu_residual_ln",
    )(h, wg, wu, wd, row(g), row(b))


def _cast_kernel(*refs):
    n = len(refs) // 2
    for src, dst in zip(refs[:n], refs[n:]):
        dst[...] = src[...].astype(dst.dtype)


CAST_STEPS = 32


def _cast_expert_weights(ws):
    flat = [w.reshape(-1, w.shape[-1]) for w in ws]
    assert all(f.shape[0] % (8 * CAST_STEPS) == 0 for f in flat)
    specs = [pl.BlockSpec((f.shape[0] // CAST_STEPS, f.shape[1]), lambda i: (i, 0)) for f in flat]
    outs = pl.pallas_call(
        _cast_kernel,
        grid=(CAST_STEPS,),
        in_specs=specs,
        out_specs=specs,
        out_shape=[jax.ShapeDtypeStruct(f.shape, BF16) for f in flat],
        compiler_params=_params("parallel"),
        name="cast_bf16",
    )(*flat)
    return [o.reshape(w.shape) for o, w in zip(outs, ws)]


Q_COLS = N_Q_HEADS * HEAD_DIM
K_COLS = 2 * N_KV_HEADS * 128
V_COLS = N_KV_HEADS * 128
KV_COLS = K_COLS + V_COLS


def _qkv_rope_kernel(h_ref, w_ref, bias_ref, cos_ref, sin_ref, q_ref, kv_ref):
    acc = jnp.dot(h_ref[...].astype(BF16), w_ref[...], preferred_element_type=F32) + bias_ref[...]
    cos = cos_ref[...]
    sin = sin_ref[...]
    lane = lax.broadcasted_iota(jnp.int32, cos.shape, 1)
    first_half = (lane % HEAD_DIM) < (HEAD_DIM // 2)
    for c0 in range(0, Q_COLS + K_COLS, 128):
        x = acc[:, c0:c0 + 128]
        swapped = jnp.where(first_half, pltpu.roll(x, 128 - HEAD_DIM // 2, 1),
                            pltpu.roll(x, HEAD_DIM // 2, 1))
        y = x * cos + swapped * sin
        if c0 < Q_COLS:
            q_ref[:, c0:c0 + 128] = (y * (HEAD_DIM ** -0.5)).astype(BF16)
        else:
            kv_ref[:, c0 - Q_COLS:c0 - Q_COLS + 128] = y.astype(BF16)
    kv_ref[:, K_COLS:] = acc[:, Q_COLS + K_COLS:].astype(BF16)


def _rope_tables(lp):
    half = HEAD_DIM // 2
    inv = ROPE_THETA ** (-jnp.arange(half, dtype=F32) / half)
    pos = (jnp.arange(lp) - FRONT_PAD).astype(F32)
    ang = pos[:, None] * inv[None, :]
    cos = jnp.cos(ang)
    sin = jnp.sin(ang)
    cos = jnp.concatenate([cos, cos] * (128 // HEAD_DIM), axis=1)
    sin = jnp.concatenate([-sin, sin] * (128 // HEAD_DIM), axis=1)
    return cos, sin


def _qkv_rope(h, w, bias, lp, tm):
    t, d = h.shape
    n = w.shape[1]
    cos, sin = _rope_tables(lp)
    per_seq = lp // tm
    return pl.pallas_call(
        _qkv_rope_kernel,
        grid=(t // tm,),
        in_specs=[pl.BlockSpec((tm, d), lambda i: (i, 0)),
                  pl.BlockSpec((d, n), lambda i: (0, 0)),
                  pl.BlockSpec((1, n), lambda i: (0, 0)),
                  pl.BlockSpec((tm, 128), lambda i: (i % per_seq, 0)),
                  pl.BlockSpec((tm, 128), lambda i: (i % per_seq, 0))],
        out_specs=[pl.BlockSpec((tm, Q_COLS), lambda i: (i, 0)),
                   pl.BlockSpec((tm, KV_COLS), lambda i: (i, 0))],
        out_shape=[jax.ShapeDtypeStruct((t, Q_COLS), BF16), jax.ShapeDtypeStruct((t, KV_COLS), BF16)],
        compiler_params=_params("parallel"),
        name="qkv_rope",
    )(h, w, bias.reshape(1, -1), cos, sin)


def _qkv_weights(w_qkv, b_qkv):
    kv_w = N_KV_HEADS * HEAD_DIM
    wq, wk, wv = jnp.split(w_qkv, [Q_COLS, Q_COLS + kv_w], axis=1)
    bq, bk, bv = jnp.split(b_qkv[None], [Q_COLS, Q_COLS + kv_w], axis=1)

    def expand(m):
        zero = jnp.zeros((m.shape[0], HEAD_DIM), m.dtype)
        k_cols, v_cols = [], []
        for g in range(N_KV_HEADS):
            col = m[:, g * HEAD_DIM:(g + 1) * HEAD_DIM]
            k_cols += [col, zero, zero, col]
            v_cols += [col, col]
        return k_cols, v_cols

    w_cols = [wq] + expand(wk)[0] + expand(wv)[1]
    b_cols = [bq] + expand(bk)[0] + expand(bv)[1]
    return jnp.concatenate(w_cols, axis=1), jnp.concatenate(b_cols, axis=1)[0]


NEG = -1e30


def _attn_kernel(sink_ref, q_ref, kv_prev_ref, kv_cur_ref, o_ref):
    n = pl.program_id(1)
    blk = ATT_BLOCK
    kv = jnp.concatenate([kv_prev_ref[...], kv_cur_ref[...]], axis=0)
    qi = lax.broadcasted_iota(jnp.int32, (blk, 2 * blk), 0)
    kj = lax.broadcasted_iota(jnp.int32, (blk, 2 * blk), 1)
    first_key = FRONT_PAD - (n - 1) * blk
    valid = (kj > qi) & (kj <= qi + blk) & (kj >= first_key)
    low = lax.broadcasted_iota(jnp.int32, (blk, 128), 1) < HEAD_DIM
    for pair in range(N_Q_HEADS // 2):
        grp = (2 * pair) // GROUP
        vd = kv[:, K_COLS + grp * 128:K_COLS + (grp + 1) * 128]
        qp = q_ref[:, pair * 128:(pair + 1) * 128]
        outs = []
        for half in range(2):
            head = 2 * pair + half
            kh = kv[:, (2 * grp + half) * 128:(2 * grp + half + 1) * 128]
            s = lax.dot_general(qp, kh, NT, preferred_element_type=F32)
            s = jnp.where(valid, s, NEG)
            sink = sink_ref[head]
            m = jnp.maximum(jnp.max(s, axis=-1, keepdims=True), sink)
            p = jnp.exp(s - m)
            denom = jnp.sum(p, axis=-1, keepdims=True) + jnp.exp(sink - m)
            o = jnp.dot(p.astype(BF16), vd, preferred_element_type=F32)
            outs.append(o * (1.0 / denom))
        o_ref[:, pair * 128:(pair + 1) * 128] = jnp.where(low, outs[0], outs[1]).astype(o_ref.dtype)


def _attention(q, kv, sinks, n_batch, lp):
    t = q.shape[0]
    nb = lp // ATT_BLOCK
    return pl.pallas_call(
        _attn_kernel,
        grid=(n_batch, nb),
        in_specs=[pl.BlockSpec(memory_space=pltpu.SMEM),
                  pl.BlockSpec((ATT_BLOCK, Q_COLS), lambda b, n: (b * nb + n, 0)),
                  pl.BlockSpec((ATT_BLOCK, KV_COLS), lambda b, n: (b * nb + jnp.maximum(n - 1, 0), 0)),
                  pl.BlockSpec((ATT_BLOCK, KV_COLS), lambda b, n: (b * nb + n, 0))],
        out_specs=pl.BlockSpec((ATT_BLOCK, Q_COLS), lambda b, n: (b * nb + n, 0)),
        out_shape=jax.ShapeDtypeStruct((t, Q_COLS), BF16),
        compiler_params=_params("parallel", "parallel"),
        name="swa_sink_attention",
    )(sinks.astype(F32), q, kv, kv)


LANE_E0, LANE_E1, LANE_W0, LANE_W1, LANE_R0, LANE_R1 = range(6)


ROUTE_ROWS = 256


def _oproj_ln_route_kernel(lp, h_ref, x_ref, w_ref, bias_ref, g_ref, b_ref, rw_ref, tri_ref,
                           o_ref, ob_ref, route_ref, count_ref, run_ref):
    @pl.when(pl.program_id(0) == 0)
    def _():
        run_ref[...] = jnp.zeros_like(run_ref)

    tm = h_ref.shape[0]
    run = run_ref[0:1, :]
    lane = lax.broadcasted_iota(jnp.int32, (ROUTE_ROWS, 128), 1)
    sub = lax.broadcasted_iota(jnp.int32, (ROUTE_ROWS, 128), 0)
    for r0 in range(0, tm, ROUTE_ROWS):
        rows = slice(r0, r0 + ROUTE_ROWS)
        acc = ALPHA * h_ref[rows, :] + bias_ref[...] + jnp.dot(
            x_ref[rows, :], w_ref[...], preferred_element_type=F32)
        hn = _layer_norm(acc, g_ref[...], b_ref[...], LN_EPS)
        o_ref[rows, :] = hn
        ob_ref[rows, :] = hn.astype(BF16)

        logits = _dot(hn, rw_ref[...], NN, passes=3)
        logits = jnp.where(lane < N_EXPERTS, logits, -jnp.inf)
        v1 = jnp.max(logits, axis=-1, keepdims=True)
        i1 = jnp.min(jnp.where(logits == v1, lane, 128), axis=-1, keepdims=True)
        rest = jnp.where(lane == i1, -jnp.inf, logits)
        v2 = jnp.max(rest, axis=-1, keepdims=True)
        i2 = jnp.min(jnp.where(rest == v2, lane, 128), axis=-1, keepdims=True)
        w1 = 1.0 / (1.0 + jnp.exp(v2 - v1))
        w2 = 1.0 - w1

        first = pl.program_id(0) * tm + r0
        lead = ((first + ROUTE_ROWS - 1) // lp) * lp
        row = first + sub
        routed = (row < lead) | (row >= lead + FRONT_PAD + N_META)

        oh1 = jnp.where(routed & (lane == i1), 1.0, 0.0)
        oh2 = jnp.where(routed & (lane == i2), 1.0, 0.0)
        both = oh1 + oh2
        before = jnp.dot(tri_ref[...], both.astype(BF16), preferred_element_type=F32) + run
        r1 = jnp.sum(oh1 * before, axis=-1, keepdims=True)
        r2 = jnp.sum(oh2 * before, axis=-1, keepdims=True)
        run = run + jnp.sum(both, axis=0, keepdims=True)

        out = jnp.where(lane == LANE_E0, i1.astype(F32), 0.0)
        out = jnp.where(lane == LANE_E1, i2.astype(F32), out)
        out = jnp.where(lane == LANE_W0, w1, out)
        out = jnp.where(lane == LANE_W1, w2, out)
        out = jnp.where(lane == LANE_R0, r1, out)
        out = jnp.where(lane == LANE_R1, r2, out)
        route_ref[rows, :] = out
    run_ref[...] = jnp.broadcast_to(run, run_ref.shape)
    count_ref[...] = jnp.broadcast_to(run, count_ref.shape)


def _oproj_ln_route(h, x, w, bias, g, b, router_w, lp, tm):
    t, d = h.shape
    assert tm % ROUTE_ROWS == 0 and lp > ROUTE_ROWS
    row = lambda v: v.reshape(1, -1)
    rw = jnp.zeros((d, 128), F32).at[:, :N_EXPERTS].set(router_w)
    tri = jnp.asarray(np.tril(np.ones((ROUTE_ROWS, ROUTE_ROWS), np.float32), -1), BF16)
    tile = lambda width: pl.BlockSpec((tm, width), lambda i: (i, 0))
    full = lambda shape: pl.BlockSpec(shape, lambda i: (0, 0))
    return pl.pallas_call(
        functools.partial(_oproj_ln_route_kernel, lp),
        grid=(t // tm,),
        in_specs=[tile(d), tile(x.shape[1]), full(w.shape), full((1, d)), full((1, d)), full((1, d)),
                  full((d, 128)), full((ROUTE_ROWS, ROUTE_ROWS))],
        out_specs=[tile(d), tile(d), tile(128), full((8, 128))],
        out_shape=[jax.ShapeDtypeStruct((t, d), F32), jax.ShapeDtypeStruct((t, d), BF16),
                   jax.ShapeDtypeStruct((t, 128), F32), jax.ShapeDtypeStruct((8, 128), F32)],
        scratch_shapes=[pltpu.VMEM((8, 128), F32)],
        compiler_params=_params("arbitrary"),
        name="oproj_ln_route",
    )(h, x, w, row(bias), row(g), row(b), rw, tri)


def _moe_kernel(n_ff, te_ref, tv_ref, x_ref, wg_ref, wu_ref, wd_ref, o_ref, acc_ref):
    i = pl.program_id(0)
    j = pl.program_id(1)
    last = n_ff - 1

    def step(first, final):
        for rows in _row_blocks(x_ref.shape[0]):
            part = _swiglu_rows(x_ref, wg_ref.at[0], wu_ref.at[0], wd_ref.at[0], rows)
            if not first:
                part = acc_ref[rows, :] + part
            if final:
                o_ref[rows, :] = part.astype(o_ref.dtype)
            else:
                acc_ref[rows, :] = part

    valid = tv_ref[i] > 0
    pl.when(valid & (j == 0))(functools.partial(step, True, False))
    if n_ff > 2:
        pl.when(valid & (j > 0) & (j < last))(functools.partial(step, False, False))
    pl.when(valid & (j == last))(functools.partial(step, False, True))


def _moe_experts(x_sorted, tile_expert, tile_valid, wg, wu, wd, tm, tf):
    n_rows, d = x_sorted.shape
    dff = wg.shape[2]
    grid_spec = pltpu.PrefetchScalarGridSpec(
        num_scalar_prefetch=2,
        grid=(n_rows // tm, dff // tf),
        in_specs=[pl.BlockSpec((tm, d), lambda i, j, te, tv: (i, 0)),
                  pl.BlockSpec((1, d, tf), lambda i, j, te, tv: (te[i], 0, j)),
                  pl.BlockSpec((1, d, tf), lambda i, j, te, tv: (te[i], 0, j)),
                  pl.BlockSpec((1, tf, d), lambda i, j, te, tv: (te[i], j, 0))],
        out_specs=pl.BlockSpec((tm, d), lambda i, j, te, tv: (i, 0)),
        scratch_shapes=[pltpu.VMEM((tm, d), F32)],
    )
    assert dff // tf >= 2
    return pl.pallas_call(
        functools.partial(_moe_kernel, dff // tf),
        grid_spec=grid_spec,
        out_shape=jax.ShapeDtypeStruct((n_rows, d), BF16),
        compiler_params=_params("parallel", "arbitrary"),
        name="moe_expert_swiglu",
    )(tile_expert, tile_valid, x_sorted, wg, wu, wd)


def _route_plan(route, counts, tm, n_batch, lp):
    lead = FRONT_PAD + N_META
    seq_route = route.reshape(n_batch, lp, -1)[:, lead:, :LANE_R1 + 1].astype(jnp.int32)
    seq_route = seq_route.reshape(-1, LANE_R1 + 1)
    top_idx = seq_route[:, LANE_E0:LANE_E1 + 1]
    rank = seq_route[:, LANE_R0:LANE_R1 + 1]
    n_assign = top_idx.size
    n_rows = n_assign + N_EXPERTS * tm
    counts = counts[0, :N_EXPERTS].astype(jnp.int32)
    padded = ((counts + tm - 1) // tm) * tm
    ends = jnp.cumsum(padded)
    starts = ends - padded
    pos = starts.at[top_idx].get(mode="promise_in_bounds") + rank
    token = (jnp.arange(n_batch, dtype=jnp.int32)[:, None] * lp + lead
             + jnp.arange(lp - lead, dtype=jnp.int32)[None, :]).reshape(-1)
    src = jnp.arange(n_rows, dtype=jnp.int32) % (n_batch * lp)
    src = src.at[pos.reshape(-1)].set(jnp.repeat(token, 2), unique_indices=True,
                                      mode="promise_in_bounds")
    tile_start = jnp.arange(n_rows // tm, dtype=jnp.int32) * tm
    tile_expert = jnp.sum((tile_start[:, None] >= ends[None, :]).astype(jnp.int32), axis=1)
    tile_expert = jnp.minimum(tile_expert, N_EXPERTS - 1)
    tile_valid = (tile_start < ends[-1]).astype(jnp.int32)
    return pos, src, tile_expert, tile_valid


def _combine_ln_kernel(h_ref, y1_ref, y2_ref, r_ref, g_ref, b_ref, o_ref):
    route = r_ref[...]
    acc = (ALPHA * h_ref[...] + route[:, LANE_W0:LANE_W0 + 1] * y1_ref[...].astype(F32)
           + route[:, LANE_W1:LANE_W1 + 1] * y2_ref[...].astype(F32))
    o_ref[...] = _layer_norm(acc, g_ref[...], b_ref[...], LN_EPS)


def _combine_ln(h, y1, y2, route, g, b, n_batch, lp, tm):
    d = h.shape[1]
    seq = lp - FRONT_PAD - N_META
    per_seq = seq // tm
    row = lambda v: v.reshape(1, -1)
    seq_tile = lambda width: pl.BlockSpec(
        (pl.Element(tm), pl.Element(width)),
        lambda b, j: (pl.multiple_of(b * lp + FRONT_PAD + N_META + j * tm, ATT_BLOCK), 0))
    tile = pl.BlockSpec((tm, d), lambda b, j: (b * per_seq + j, 0))
    vec = pl.BlockSpec((1, d), lambda b, j: (0, 0))
    return pl.pallas_call(
        _combine_ln_kernel,
        grid=(n_batch, per_seq),
        in_specs=[seq_tile(d), tile, tile, seq_tile(128), vec, vec],
        out_specs=tile,
        out_shape=jax.ShapeDtypeStruct((n_batch * seq, d), F32),
        compiler_params=_params("parallel", "parallel"),
        name="moe_combine_ln",
    )(h, y1, y2, route, row(g), row(b))


def _pick_tile(n, target, mult=16):
    best = mult
    for cand in range(mult, min(n, target) + 1, mult):
        if n % cand == 0:
            best = cand
    return best


def kernel(x, meta_tokens, ev_w_in, ev_conv_w, ev_conv_b, ev_convnorm_g, ev_convnorm_b, ev_shift_mu, ev_w0, ev_w2, ev_a0, ev_a2, ev_g2, ev_k_k, ev_k_a, ev_r_k, ev_lnx_g, ev_lnx_b, ev_w_out, ev_ln1_g, ev_ln1_b, ev_ffn_gate, ev_ffn_up, ev_ffn_down, ev_ln2_g, ev_ln2_b, od_w_qkv, od_b_qkv, od_sinks, od_w_o, od_b_o, od_ln1_g, od_ln1_b, od_router, od_exp_gate, od_exp_up, od_exp_down, od_ln2_g, od_ln2_b):
    n_batch, seq, d = x.shape
    lp = FRONT_PAD + N_META + seq
    assert d == D_MODEL and lp % ATT_BLOCK == 0 and lp % (CHUNK * RWKV_CHUNKS_PER_STEP) == 0
    t = n_batch * lp
    tm = _pick_tile(t, 1024)
    tm_seq = _pick_tile(lp, 1056)
    assert t % CONV_ROWS == 0

    meta = jnp.broadcast_to(meta_tokens[None].astype(x.dtype), (n_batch, N_META, d))
    h = jnp.concatenate([jnp.zeros((n_batch, FRONT_PAD, d), x.dtype), meta, x], axis=1)
    h = h.reshape(t, d)

    w_in = ev_w_in[0].astype(BF16)
    a_out = _conv_module(h, w_in[:, :2 * CONV_CH], ev_conv_w[0], ev_conv_b[0], ev_convnorm_g[0],
                         ev_convnorm_b[0])
    b_out = _rwkv_time_mix(h, w_in[:, 2 * CONV_CH:], n_batch, ev_shift_mu[0], ev_w0[0], ev_w2[0],
                           ev_a0[0], ev_a2[0], ev_g2[0], ev_k_k[0], ev_k_a[0], ev_r_k[0],
                           ev_lnx_g[0], ev_lnx_b[0])
    w_out = ev_w_out[0].astype(BF16)
    h = _proj_ln(h, [a_out, b_out], [w_out[:CONV_CH], w_out[CONV_CH:]], None,
                 ev_ln1_g[0], ev_ln1_b[0], tm)
    h = _ffn_ln(h, ev_ffn_gate[0].astype(BF16), ev_ffn_up[0].astype(BF16),
                ev_ffn_down[0].astype(BF16), ev_ln2_g[0], ev_ln2_b[0],
                _pick_tile(t, 512, SWIGLU_ROWS))

    w_qkv, b_qkv = _qkv_weights(od_w_qkv[0], od_b_qkv[0])
    q, kv = _qkv_rope(h, w_qkv.astype(BF16), b_qkv, lp, tm_seq)
    att = _attention(q, kv, od_sinks[0], n_batch, lp)
    h, h_bf16, route, counts = _oproj_ln_route(h, att, od_w_o[0].astype(BF16), od_b_o[0], od_ln1_g[0],
                                               od_ln1_b[0], od_router[0], lp, tm)
    tm_moe = _pick_tile(2 * n_batch * seq, 1024, SWIGLU_ROWS)
    pos, src, tile_expert, tile_valid = _route_plan(route, counts, tm_moe, n_batch, lp)
    x_sorted = h_bf16.at[src].get(mode="promise_in_bounds")
    exp_w = _cast_expert_weights([od_exp_gate[0], od_exp_up[0], od_exp_down[0]])
    y_sorted = _moe_experts(x_sorted, tile_expert, tile_valid, *exp_w, tm_moe, 1792)
    y1 = y_sorted.at[pos[:, 0]].get(mode="promise_in_bounds", unique_indices=True)
    y2 = y_sorted.at[pos[:, 1]].get(mode="promise_in_bounds", unique_indices=True)
    out = _combine_ln(h, y1, y2, route, od_ln2_g[0], od_ln2_b[0], n_batch, lp, _pick_tile(seq, 1024))
    return out.reshape(n_batch, seq, d)
```

```python
import functools
import math

import jax
import jax.numpy as jnp
import numpy as np
from jax import lax
from jax.experimental import pallas as pl
from jax.experimental.pallas import tpu as pltpu

F32 = jnp.float32
BF16 = jnp.bfloat16

D_MODEL = 1024
N_META = 16
ATT_BLOCK = 128
FRONT_PAD = (-N_META) % ATT_BLOCK
CONV_CH = 512
CONV_WIDTH = 31
RWKV_CH = 512
RWKV_HEAD = 64
RWKV_COLS = 3 * RWKV_CH + 64 + 64 + 128
HEAD_DIM = 64
N_Q_HEADS = 16
N_KV_HEADS = 2
GROUP = N_Q_HEADS // N_KV_HEADS
ROPE_THETA = 10000.0
N_EXPERTS = 8
DEPTH = 2
ALPHA = (2 * DEPTH) ** 0.25
LN_EPS = 1e-5
LNX_EPS = 64e-5

CHUNK = 64
RWKV_CHUNKS_PER_STEP = 3
HEADS_PER_GROUP = 2
GROUP_W = HEADS_PER_GROUP * RWKV_HEAD
VMEM_LIMIT = 56 * 1024 * 1024

NN = (((1,), (0,)), ((), ()))
NT = (((1,), (1,)), ((), ()))
TN = (((0,), (0,)), ((), ()))


def _params(*sem):
    return pltpu.CompilerParams(dimension_semantics=sem, vmem_limit_bytes=VMEM_LIMIT)


def _sigmoid(x):
    return 1.0 / (1.0 + jnp.exp(-x))


def _split2(x):
    hi = x.astype(BF16)
    lo = (x - hi.astype(F32)).astype(BF16)
    return hi, lo


def _dot(a, b, dims=NN, passes=1):
    if passes == 1:
        return lax.dot_general(a.astype(BF16), b.astype(BF16), dims, preferred_element_type=F32)
    ah, al = _split2(a)
    bh, bl = _split2(b)
    out = lax.dot_general(ah, bh, dims, preferred_element_type=F32)
    out = out + lax.dot_general(ah, bl, dims, preferred_element_type=F32)
    return out + lax.dot_general(al, bh, dims, preferred_element_type=F32)


def _dot_exact_rhs(a, b_bf16, dims=NN, parts=2):
    out = None
    rem = a
    for _ in range(parts):
        piece = rem.astype(BF16)
        term = lax.dot_general(piece, b_bf16, dims, preferred_element_type=F32)
        out = term if out is None else out + term
        rem = rem - piece.astype(F32)
    return out


def _layer_norm(x, g, b, eps):
    mu = jnp.mean(x, axis=-1, keepdims=True)
    xc = x - mu
    var = jnp.mean(xc * xc, axis=-1, keepdims=True)
    return xc * lax.rsqrt(var + eps) * g + b


CONV_ROWS = 256
CONV_SUB = 32
CONV_HALO = 32
SUBLANES = 8


def _conv_kernel(h_ref, w_ref, cw_ref, cb_ref, g_ref, b_ref, o_ref, buf_ref, sh_ref):
    @pl.when(pl.program_id(0) == 0)
    def _():
        buf_ref[0:CONV_HALO, :] = jnp.zeros((CONV_HALO, CONV_CH), F32)

    u = jnp.dot(h_ref[...].astype(BF16), w_ref[...], preferred_element_type=F32)
    buf_ref[CONV_HALO:CONV_HALO + CONV_ROWS, :] = u[:, :CONV_CH] * _sigmoid(u[:, CONV_CH:])
    n_sh = CONV_HALO + CONV_ROWS - SUBLANES
    for r in range(1, SUBLANES):
        sh_ref[r - 1] = buf_ref[r:r + n_sh, :]
    cw = cw_ref[...]
    base = CONV_HALO - (CONV_WIDTH - 1)
    for s in range(CONV_ROWS // CONV_SUB):
        r0 = s * CONV_SUB
        acc = jnp.broadcast_to(cb_ref[...], (CONV_SUB, CONV_CH))
        for j in range(CONV_WIDTH):
            shift = (base + j) % SUBLANES
            start = base + j - shift + r0
            if shift == 0:
                win = buf_ref[start:start + CONV_SUB, :]
            else:
                win = sh_ref[shift - 1, start:start + CONV_SUB, :]
            acc = acc + cw[j:j + 1, :] * win
        y = _layer_norm(acc, g_ref[...], b_ref[...], LN_EPS)
        o_ref[r0:r0 + CONV_SUB, :] = (y * _sigmoid(y)).astype(o_ref.dtype)
    buf_ref[0:CONV_HALO, :] = buf_ref[CONV_ROWS:CONV_ROWS + CONV_HALO, :]


def _conv_module(h, w_in_conv, conv_w, conv_b, norm_g, norm_b):
    t, d = h.shape
    row = lambda v: v.reshape(1, -1)
    full = lambda shape: pl.BlockSpec(shape, lambda i: (0, 0))
    return pl.pallas_call(
        _conv_kernel,
        grid=(t // CONV_ROWS,),
        in_specs=[pl.BlockSpec((CONV_ROWS, d), lambda i: (i, 0)), full((d, 2 * CONV_CH)),
                  full((CONV_WIDTH, CONV_CH)), full((1, CONV_CH)), full((1, CONV_CH)),
                  full((1, CONV_CH))],
        out_specs=pl.BlockSpec((CONV_ROWS, CONV_CH), lambda i: (i, 0)),
        out_shape=jax.ShapeDtypeStruct((t, CONV_CH), BF16),
        scratch_shapes=[pltpu.VMEM((CONV_ROWS + CONV_HALO, CONV_CH), F32),
                        pltpu.VMEM((SUBLANES - 1, CONV_ROWS + CONV_HALO - SUBLANES, CONV_CH), F32)],
        compiler_params=_params("arbitrary"),
        name="conv_module",
    )(h, w_in_conv, conv_w, row(conv_b), row(norm_g), row(norm_b))


def _rwkv_masks(step_rows):
    n = GROUP_W
    r = np.arange(n)[:, None]
    c = np.arange(n)[None, :]
    same = lambda w: (r // w) == (c // w)
    strict = (r > c) & same(CHUNK)
    incl = (r >= c) & same(CHUNK)
    m16 = (r > c) & same(16)
    m32 = (r > c) & same(32) & ~same(16)
    m64 = (r > c) & same(64) & ~same(32)
    stack = (c // RWKV_HEAD) == (r // CHUNK)
    masks = np.stack([strict, incl, m16, m32, m64, stack]).astype(np.float32)
    tr = np.arange(step_rows)
    tri = ((tr[:, None] >= tr[None, :]) & ((tr[:, None] // CHUNK) == (tr[None, :] // CHUNK)))
    tri = tri.astype(np.float32)
    hr = np.arange(RWKV_CH)
    head_ones = ((hr[:, None] // RWKV_HEAD) == (hr[None, :] // RWKV_HEAD)).astype(np.float32)
    return masks, tri, head_ones


def _each(f, *cols):
    return [f(*xs) for xs in zip(*cols)]


def _unit_lower_inverse_minus_identity(labs, m16, m32, m64, passes):
    mm = functools.partial(_dot, passes=passes)
    d = [lab * m16 for lab in labs]
    d2 = _each(mm, d, d)
    d4 = _each(mm, d2, d2)
    dd2 = _each(mm, d, d2)
    d8 = _each(mm, d4, d4)
    n = _each(lambda x, x2, xx2: x2 - x - xx2, d, d2, dd2)
    n = _each(lambda x, y, xy: x + y + xy, n, d4, _each(mm, n, d4))
    n = _each(lambda x, y, xy: x + y + xy, n, d8, _each(mm, n, d8))
    for mask in (m32, m64):
        off = [lab * mask for lab in labs]
        w = _each(lambda o, no: o + no, off, _each(mm, n, off))
        w = _each(lambda x, xn: x + xn, w, _each(mm, w, n))
        n = _each(lambda x, y: x - y, n, w)
    return n


def _rwkv_kernel(h_ref, win_ref, mu_ref, w0_ref, w2_ref, a0_ref, a2_ref, g2_ref, kk_ref, ka_ref,
                 rk_ref, lg_ref, lb_ref, masks_ref, tri_ref, ones_ref, o_ref, s_ref, carry_ref):
    c = CHUNK
    rows = h_ref.shape[0]
    n_groups = RWKV_CH // GROUP_W
    w = GROUP_W

    @pl.when(pl.program_id(1) == 0)
    def _():
        s_ref[...] = jnp.zeros_like(s_ref)
        carry_ref[...] = jnp.zeros_like(carry_ref)

    z = jnp.dot(h_ref[...].astype(BF16), win_ref[...], preferred_element_type=F32)
    prev = pltpu.roll(z, 1, 0)
    row_id = lax.broadcasted_iota(jnp.int32, z.shape, 0)
    prev = jnp.where(row_id == 0, jnp.broadcast_to(carry_ref[7:8, :], z.shape), prev)
    carry_ref[...] = z[rows - 8:rows, :]
    z = z + (prev - z) * mu_ref[...]

    r = z[:, 0:RWKV_CH]
    k = z[:, RWKV_CH:2 * RWKV_CH]
    v = z[:, 2 * RWKV_CH:3 * RWKV_CH]
    wa_lo = z[:, 3 * RWKV_CH:3 * RWKV_CH + 128]
    g_lo = z[:, 3 * RWKV_CH + 128:3 * RWKV_CH + 256]

    wpre = w0_ref[...] + _dot(jnp.tanh(wa_lo), w2_ref[...], passes=3)
    ew = _sigmoid(wpre) * math.exp(-0.5)
    a = _sigmoid(a0_ref[...] + _dot(wa_lo, a2_ref[...]))
    gate = _dot(_sigmoid(g_lo), g2_ref[...])

    ones = ones_ref[...]
    kk = k * kk_ref[...]
    ss = _dot_exact_rhs(kk * kk, ones, parts=1)
    kk = kk * lax.rsqrt(jnp.maximum(ss, 1e-24))
    k2 = k * (1.0 + (a - 1.0) * ka_ref[...])
    kka = kk * a

    e0 = ew.astype(BF16)
    e1f = ew - e0.astype(F32)
    e1 = e1f.astype(BF16)
    e2 = (e1f - e1.astype(F32)).astype(BF16)
    tri = tri_ref[...]
    cum = (jnp.dot(tri, e0, preferred_element_type=F32) + jnp.dot(tri, e1, preferred_element_type=F32)
           + jnp.dot(tri, e2, preferred_element_type=F32))
    gam = jnp.exp(-cum)
    igam = jnp.exp(cum)
    r_t = r * gam
    ka_t = kk * jnp.exp(ew - cum)
    b_t = kka * igam
    k_t = k2 * igam

    strict, incl, m16, m32, m64, stack = (masks_ref[i] for i in range(6))
    p1 = 1
    mm = functools.partial(_dot, passes=p1)

    units = [(q, g) for q in range(rows // c) for g in range(n_groups)]

    def stacked(x):
        return [jnp.concatenate([x[q * c:(q + 1) * c, g * w:(g + 1) * w]] * HEADS_PER_GROUP, axis=0)
                * stack for q, g in units]

    xa, xr, xv, yb, yk = stacked(ka_t), stacked(r_t), stacked(v), stacked(b_t), stacked(k_t)
    xar = _each(lambda top, bot: jnp.concatenate([top, bot], axis=0), xa, xr)
    pb = _each(lambda x, y: mm(x, y, NT), xar, yb)
    pk = _each(lambda x, y: mm(x, y, NT), xar, yk)
    lab = [p[:w] * strict for p in pb]
    arb = [p[w:] * incl for p in pb]
    lak = [p[:w] * strict for p in pk]
    ark = [p[w:] * incl for p in pk]
    n_inv = _unit_lower_inverse_minus_identity(lab, m16, m32, m64, p1)
    lv = _each(mm, lak, xv)
    rhs = _each(lambda left, right: jnp.concatenate([left, right], axis=1), xa, lv)
    gz = _each(lambda x, nx: x + nx, rhs, _each(mm, n_inv, rhs))
    corr = _each(mm, arb, gz)
    qh = _each(lambda x, cr: x - cr[:, :w], xr, corr)
    yloc = _each(lambda av, cr: av - cr[:, w:], _each(mm, ark, xv), corr)
    gzb = _each(lambda x, y: mm(x, y, TN), gz, yb)
    vk = _each(lambda x, y: mm(x, y, TN), xv, yk)

    state = [s_ref[g] for g in range(n_groups)]
    y_rows = []
    for q in range(rows // c):
        gam_end = gam[(q + 1) * c - 1:(q + 1) * c, :]
        ys = []
        for g in range(n_groups):
            i = q * n_groups + g
            s0 = state[g]
            y_st = mm(qh[i], s0, NT) + yloc[i]
            y_heads = y_st[0:c]
            for h in range(1, HEADS_PER_GROUP):
                y_heads = y_heads + y_st[h * c:(h + 1) * c]
            ys.append(y_heads)
            s_new = s0 - mm(s0, gzb[i][:w]) + vk[i] - gzb[i][w:]
            state[g] = s_new * gam_end[:, g * w:(g + 1) * w]
        y_rows.append(jnp.concatenate(ys, axis=1))
    for g in range(n_groups):
        s_ref[g] = state[g]
    y = jnp.concatenate(y_rows, axis=0)

    inv_n = 1.0 / RWKV_HEAD
    mean = _dot_exact_rhs(y, ones) * inv_n
    yc = y - mean
    var = _dot_exact_rhs(yc * yc, ones, parts=1) * inv_n
    yn = yc * lax.rsqrt(var + LNX_EPS) * lg_ref[...] + lb_ref[...]
    bonus = _dot_exact_rhs(r * k2 * rk_ref[...], ones, parts=1) * v
    o_ref[...] = ((yn + bonus) * gate).astype(o_ref.dtype)


def _rwkv_time_mix(h, w_in_rwkv, n_batch, shift_mu, w0, w2, a0, a2, g2, k_k, k_a, r_k, lnx_g, lnx_b):
    t, d = h.shape
    step_rows = CHUNK * RWKV_CHUNKS_PER_STEP
    n_steps = t // n_batch // step_rows
    masks, tri, head_ones = _rwkv_masks(step_rows)
    row = lambda v: v.reshape(1, -1).astype(F32)
    zeros64 = jnp.zeros((64, RWKV_CH), F32)
    w2p = jnp.concatenate([w2, zeros64], axis=0)
    a2p = jnp.concatenate([zeros64, a2], axis=0)
    full2 = lambda shape: pl.BlockSpec(shape, lambda b, c: (0, 0))
    vec = full2((1, RWKV_CH))
    return pl.pallas_call(
        _rwkv_kernel,
        grid=(n_batch, n_steps),
        in_specs=[pl.BlockSpec((step_rows, d), lambda b, c: (b * n_steps + c, 0)),
                  full2((d, RWKV_COLS)), full2((1, RWKV_COLS)), vec, full2((128, RWKV_CH)), vec, full2((128, RWKV_CH)),
                  full2((128, RWKV_CH)), vec, vec, vec, vec, vec,
                  pl.BlockSpec(masks.shape, lambda b, c: (0, 0, 0)),
                  full2((step_rows, step_rows)), full2((RWKV_CH, RWKV_CH))],
        out_specs=pl.BlockSpec((step_rows, RWKV_CH), lambda b, c: (b * n_steps + c, 0)),
        out_shape=jax.ShapeDtypeStruct((t, RWKV_CH), BF16),
        scratch_shapes=[pltpu.VMEM((RWKV_CH // GROUP_W, GROUP_W, GROUP_W), F32),
                        pltpu.VMEM((8, RWKV_COLS), F32)],
        compiler_params=_params("arbitrary", "arbitrary"),
        name="rwkv7_time_mix",
    )(h, w_in_rwkv, row(shift_mu), row(w0), w2p, row(a0), a2p, g2, row(k_k), row(k_a), row(r_k),
      row(lnx_g), row(lnx_b), jnp.asarray(masks), jnp.asarray(tri, BF16),
      jnp.asarray(head_ones, BF16))


def _proj_ln_kernel(n_in, has_bias, *refs):
    h_ref = refs[0]
    xs = refs[1:1 + n_in]
    ws = refs[1 + n_in:1 + 2 * n_in]
    rest = refs[1 + 2 * n_in:]
    if has_bias:
        bias_ref, g_ref, b_ref, o_ref = rest
    else:
        g_ref, b_ref, o_ref = rest
    acc = ALPHA * h_ref[...]
    for x_ref, w_ref in zip(xs, ws):
        acc = acc + jnp.dot(x_ref[...].astype(BF16), w_ref[...], preferred_element_type=F32)
    if has_bias:
        acc = acc + bias_ref[...]
    o_ref[...] = _layer_norm(acc, g_ref[...], b_ref[...], LN_EPS)


def _proj_ln(h, xs, ws, bias, g, b, tm):
    t, d = h.shape
    row = lambda v: v.reshape(1, -1)
    full = lambda shape: pl.BlockSpec(shape, lambda i: (0, 0))
    in_specs = [pl.BlockSpec((tm, d), lambda i: (i, 0))]
    in_specs += [pl.BlockSpec((tm, x.shape[1]), lambda i: (i, 0)) for x in xs]
    in_specs += [full(w.shape) for w in ws]
    args = [h, *xs, *ws]
    if bias is not None:
        in_specs.append(full((1, d)))
        args.append(row(bias))
    in_specs += [full((1, d)), full((1, d))]
    args += [row(g), row(b)]
    return pl.pallas_call(
        functools.partial(_proj_ln_kernel, len(xs), bias is not None),
        grid=(t // tm,),
        in_specs=in_specs,
        out_specs=pl.BlockSpec((tm, d), lambda i: (i, 0)),
        out_shape=jax.ShapeDtypeStruct((t, d), F32),
        compiler_params=_params("parallel"),
        name="proj_residual_ln",
    )(*args)


SWIGLU_ROWS = 256


def _swiglu_rows(x_ref, wg_ref, wu_ref, wd_ref, rows):
    xb = x_ref[rows, :].astype(BF16)
    gate = jnp.dot(xb, wg_ref[...], preferred_element_type=F32)
    up = jnp.dot(xb, wu_ref[...], preferred_element_type=F32)
    act = (gate * _sigmoid(gate) * up).astype(BF16)
    return jnp.dot(act, wd_ref[...], preferred_element_type=F32)


def _row_blocks(n_rows):
    assert n_rows % SWIGLU_ROWS == 0
    return [slice(r0, r0 + SWIGLU_ROWS) for r0 in range(0, n_rows, SWIGLU_ROWS)]


def _ffn_ln_kernel(h_ref, wg_ref, wu_ref, wd_ref, g_ref, b_ref, o_ref):
    for rows in _row_blocks(h_ref.shape[0]):
        acc = ALPHA * h_ref[rows, :] + _swiglu_rows(h_ref, wg_ref, wu_ref, wd_ref, rows)
        o_ref[rows, :] = _layer_norm(acc, g_ref[...], b_ref[...], LN_EPS)


def _ffn_ln(h, wg, wu, wd, g, b, tm):
    t, d = h.shape
    dff = wg.shape[1]
    row = lambda v: v.reshape(1, -1)
    full = lambda shape: pl.BlockSpec(shape, lambda i: (0, 0))
    return pl.pallas_call(
        _ffn_ln_kernel,
        grid=(t // tm,),
        in_specs=[pl.BlockSpec((tm, d), lambda i: (i, 0)), full((d, dff)), full((d, dff)),
                  full((dff, d)), full((1, d)), full((1, d))],
        out_specs=pl.BlockSpec((tm, d), lambda i: (i, 0)),
        out_shape=jax.ShapeDtypeStruct((t, d), F32),
        compiler_params=_params("parallel"),
        name="swiglu_residual_ln",
    )(h, wg, wu, wd, row(g), row(b))


def _cast_kernel(*refs):
    n = len(refs) // 2
    for src, dst in zip(refs[:n], refs[n:]):
        dst[...] = src[...].astype(dst.dtype)


CAST_STEPS = 32


def _cast_expert_weights(ws):
    flat = [w.reshape(-1, w.shape[-1]) for w in ws]
    assert all(f.shape[0] % (8 * CAST_STEPS) == 0 for f in flat)
    specs = [pl.BlockSpec((f.shape[0] // CAST_STEPS, f.shape[1]), lambda i: (i, 0)) for f in flat]
    outs = pl.pallas_call(
        _cast_kernel,
        grid=(CAST_STEPS,),
        in_specs=specs,
        out_specs=specs,
        out_shape=[jax.ShapeDtypeStruct(f.shape, BF16) for f in flat],
        compiler_params=_params("parallel"),
        name="cast_bf16",
    )(*flat)
    return [o.reshape(w.shape) for o, w in zip(outs, ws)]


Q_COLS = N_Q_HEADS * HEAD_DIM
K_COLS = 2 * N_KV_HEADS * 128
V_COLS = N_KV_HEADS * 128
KV_COLS = K_COLS + V_COLS


def _qkv_rope_kernel(h_ref, w_ref, bias_ref, cos_ref, sin_ref, q_ref, kv_ref):
    acc = jnp.dot(h_ref[...].astype(BF16), w_ref[...], preferred_element_type=F32) + bias_ref[...]
    cos = cos_ref[...]
    sin = sin_ref[...]
    lane = lax.broadcasted_iota(jnp.int32, cos.shape, 1)
    first_half = (lane % HEAD_DIM) < (HEAD_DIM // 2)
    for c0 in range(0, Q_COLS + K_COLS, 128):
        x = acc[:, c0:c0 + 128]
        swapped = jnp.where(first_half, pltpu.roll(x, 128 - HEAD_DIM // 2, 1),
                            pltpu.roll(x, HEAD_DIM // 2, 1))
        y = x * cos + swapped * sin
        if c0 < Q_COLS:
            q_ref[:, c0:c0 + 128] = (y * (HEAD_DIM ** -0.5)).astype(BF16)
        else:
            kv_ref[:, c0 - Q_COLS:c0 - Q_COLS + 128] = y.astype(BF16)
    kv_ref[:, K_COLS:] = acc[:, Q_COLS + K_COLS:].astype(BF16)


def _rope_tables(lp):
    half = HEAD_DIM // 2
    inv = ROPE_THETA ** (-jnp.arange(half, dtype=F32) / half)
    pos = (jnp.arange(lp) - FRONT_PAD).astype(F32)
    ang = pos[:, None] * inv[None, :]
    cos = jnp.cos(ang)
    sin = jnp.sin(ang)
    cos = jnp.concatenate([cos, cos] * (128 // HEAD_DIM), axis=1)
    sin = jnp.concatenate([-sin, sin] * (128 // HEAD_DIM), axis=1)
    return cos, sin


def _qkv_rope(h, w, bias, lp, tm):
    t, d = h.shape
    n = w.shape[1]
    cos, sin = _rope_tables(lp)
    per_seq = lp // tm
    return pl.pallas_call(
        _qkv_rope_kernel,
        grid=(t // tm,),
        in_specs=[pl.BlockSpec((tm, d), lambda i: (i, 0)),
                  pl.BlockSpec((d, n), lambda i: (0, 0)),
                  pl.BlockSpec((1, n), lambda i: (0, 0)),
                  pl.BlockSpec((tm, 128), lambda i: (i % per_seq, 0)),
                  pl.BlockSpec((tm, 128), lambda i: (i % per_seq, 0))],
        out_specs=[pl.BlockSpec((tm, Q_COLS), lambda i: (i, 0)),
                   pl.BlockSpec((tm, KV_COLS), lambda i: (i, 0))],
        out_shape=[jax.ShapeDtypeStruct((t, Q_COLS), BF16), jax.ShapeDtypeStruct((t, KV_COLS), BF16)],
        compiler_params=_params("parallel"),
        name="qkv_rope",
    )(h, w, bias.reshape(1, -1), cos, sin)


def _qkv_weights(w_qkv, b_qkv):
    kv_w = N_KV_HEADS * HEAD_DIM
    wq, wk, wv = jnp.split(w_qkv, [Q_COLS, Q_COLS + kv_w], axis=1)
    bq, bk, bv = jnp.split(b_qkv[None], [Q_COLS, Q_COLS + kv_w], axis=1)

    def expand(m):
        zero = jnp.zeros((m.shape[0], HEAD_DIM), m.dtype)
        k_cols, v_cols = [], []
        for g in range(N_KV_HEADS):
            col = m[:, g * HEAD_DIM:(g + 1) * HEAD_DIM]
            k_cols += [col, zero, zero, col]
            v_cols += [col, col]
        return k_cols, v_cols

    w_cols = [wq] + expand(wk)[0] + expand(wv)[1]
    b_cols = [bq] + expand(bk)[0] + expand(bv)[1]
    return jnp.concatenate(w_cols, axis=1), jnp.concatenate(b_cols, axis=1)[0]


NEG = -1e30


def _attn_kernel(sink_ref, q_ref, kv_prev_ref, kv_cur_ref, o_ref):
    n = pl.program_id(1)
    blk = ATT_BLOCK
    kv = jnp.concatenate([kv_prev_ref[...], kv_cur_ref[...]], axis=0)
    qi = lax.broadcasted_iota(jnp.int32, (blk, 2 * blk), 0)
    kj = lax.broadcasted_iota(jnp.int32, (blk, 2 * blk), 1)
    first_key = FRONT_PAD - (n - 1) * blk
    valid = (kj > qi) & (kj <= qi + blk) & (kj >= first_key)
    low = lax.broadcasted_iota(jnp.int32, (blk, 128), 1) < HEAD_DIM
    for pair in range(N_Q_HEADS // 2):
        grp = (2 * pair) // GROUP
        vd = kv[:, K_COLS + grp * 128:K_COLS + (grp + 1) * 128]
        qp = q_ref[:, pair * 128:(pair + 1) * 128]
        outs = []
        for half in range(2):
            head = 2 * pair + half
            kh = kv[:, (2 * grp + half) * 128:(2 * grp + half + 1) * 128]
            s = lax.dot_general(qp, kh, NT, preferred_element_type=F32)
            s = jnp.where(valid, s, NEG)
            sink = sink_ref[head]
            m = jnp.maximum(jnp.max(s, axis=-1, keepdims=True), sink)
            p = jnp.exp(s - m)
            denom = jnp.sum(p, axis=-1, keepdims=True) + jnp.exp(sink - m)
            o = jnp.dot(p.astype(BF16), vd, preferred_element_type=F32)
            outs.append(o * (1.0 / denom))
        o_ref[:, pair * 128:(pair + 1) * 128] = jnp.where(low, outs[0], outs[1]).astype(o_ref.dtype)


def _attention(q, kv, sinks, n_batch, lp):
    t = q.shape[0]
    nb = lp // ATT_BLOCK
    return pl.pallas_call(
        _attn_kernel,
        grid=(n_batch, nb),
        in_specs=[pl.BlockSpec(memory_space=pltpu.SMEM),
                  pl.BlockSpec((ATT_BLOCK, Q_COLS), lambda b, n: (b * nb + n, 0)),
                  pl.BlockSpec((ATT_BLOCK, KV_COLS), lambda b, n: (b * nb + jnp.maximum(n - 1, 0), 0)),
                  pl.BlockSpec((ATT_BLOCK, KV_COLS), lambda b, n: (b * nb + n, 0))],
        out_specs=pl.BlockSpec((ATT_BLOCK, Q_COLS), lambda b, n: (b * nb + n, 0)),
        out_shape=jax.ShapeDtypeStruct((t, Q_COLS), BF16),
        compiler_params=_params("parallel", "parallel"),
        name="swa_sink_attention",
    )(sinks.astype(F32), q, kv, kv)


ROW_E0, ROW_E1, ROW_W0, ROW_W1, ROW_R0, ROW_R1 = range(6)
ROUTE_FIELDS = 16
ROUTE_ROWS = 256


def _oproj_ln_route_kernel(lp, h_ref, x_ref, w_ref, bias_ref, g_ref, b_ref, rwt_ref, tri_ref,
                           o_ref, ob_ref, route_ref, count_ref, run_ref):
    @pl.when(pl.program_id(0) == 0)
    def _():
        run_ref[...] = jnp.zeros_like(run_ref)

    tm = h_ref.shape[0]
    for r0 in range(0, tm, ROUTE_ROWS):
        rows = slice(r0, r0 + ROUTE_ROWS)
        acc = ALPHA * h_ref[rows, :] + bias_ref[...] + jnp.dot(
            x_ref[rows, :], w_ref[...], preferred_element_type=F32)
        hn = _layer_norm(acc, g_ref[...], b_ref[...], LN_EPS)
        o_ref[rows, :] = hn
        ob_ref[rows, :] = hn.astype(BF16)

    sub = lax.broadcasted_iota(jnp.int32, (ROUTE_FIELDS, tm), 0)
    lane = lax.broadcasted_iota(jnp.int32, (ROUTE_FIELDS, tm), 1)
    logits = _dot(rwt_ref[...], o_ref[...], NT, passes=3)
    logits = jnp.where(sub < N_EXPERTS, logits, -jnp.inf)
    v1 = jnp.max(logits, axis=0, keepdims=True)
    i1 = jnp.min(jnp.where(logits == v1, sub, ROUTE_FIELDS), axis=0, keepdims=True)
    rest = jnp.where(sub == i1, -jnp.inf, logits)
    v2 = jnp.max(rest, axis=0, keepdims=True)
    i2 = jnp.min(jnp.where(rest == v2, sub, ROUTE_FIELDS), axis=0, keepdims=True)
    w1 = 1.0 / (1.0 + jnp.exp(v2 - v1))
    w2 = 1.0 - w1

    first = pl.program_id(0) * tm
    lead = ((first + tm - 1) // lp) * lp
    row = first + lane
    routed = (row < lead) | (row >= lead + FRONT_PAD + N_META)

    run = run_ref[:, 0:1]
    oh1 = jnp.where(routed & (sub == i1), 1.0, 0.0)
    oh2 = jnp.where(routed & (sub == i2), 1.0, 0.0)
    both = oh1 + oh2
    before = jnp.dot(both.astype(BF16), tri_ref[...], preferred_element_type=F32) + run
    r1 = jnp.sum(oh1 * before, axis=0, keepdims=True)
    r2 = jnp.sum(oh2 * before, axis=0, keepdims=True)
    run = run + jnp.sum(both, axis=1, keepdims=True)

    out = jnp.where(sub == ROW_E0, i1.astype(F32), 0.0)
    out = jnp.where(sub == ROW_E1, i2.astype(F32), out)
    out = jnp.where(sub == ROW_W0, w1, out)
    out = jnp.where(sub == ROW_W1, w2, out)
    out = jnp.where(sub == ROW_R0, r1, out)
    out = jnp.where(sub == ROW_R1, r2, out)
    route_ref[...] = out
    run_ref[...] = jnp.broadcast_to(run, run_ref.shape)
    count_ref[...] = jnp.broadcast_to(run, count_ref.shape)


def _oproj_ln_route(h, x, w, bias, g, b, router_w, lp, tm):
    t, d = h.shape
    assert tm % ROUTE_ROWS == 0 and lp > tm
    row = lambda v: v.reshape(1, -1)
    rwt = jnp.zeros((ROUTE_FIELDS, d), F32).at[:N_EXPERTS].set(router_w.T)
    tri = jnp.asarray(np.triu(np.ones((tm, tm), np.float32), 1), BF16)
    tile = lambda width: pl.BlockSpec((tm, width), lambda i: (i, 0))
    full = lambda shape: pl.BlockSpec(shape, lambda i: (0, 0))
    return pl.pallas_call(
        functools.partial(_oproj_ln_route_kernel, lp),
        grid=(t // tm,),
        in_specs=[tile(d), tile(x.shape[1]), full(w.shape), full((1, d)), full((1, d)), full((1, d)),
                  full((ROUTE_FIELDS, d)), full((tm, tm))],
        out_specs=[tile(d), tile(d), pl.BlockSpec((ROUTE_FIELDS, tm), lambda i: (0, i)),
                   full((ROUTE_FIELDS, 128))],
        out_shape=[jax.ShapeDtypeStruct((t, d), F32), jax.ShapeDtypeStruct((t, d), BF16),
                   jax.ShapeDtypeStruct((ROUTE_FIELDS, t), F32),
                   jax.ShapeDtypeStruct((ROUTE_FIELDS, 128), F32)],
        scratch_shapes=[pltpu.VMEM((ROUTE_FIELDS, 128), F32)],
        compiler_params=_params("arbitrary"),
        name="oproj_ln_route",
    )(h, x, w, row(bias), row(g), row(b), rwt, tri)


def _moe_kernel(n_ff, te_ref, tv_ref, x_ref, wg_ref, wu_ref, wd_ref, o_ref, acc_ref):
    i = pl.program_id(0)
    j = pl.program_id(1)
    last = n_ff - 1

    def step(first, final):
        for rows in _row_blocks(x_ref.shape[0]):
            part = _swiglu_rows(x_ref, wg_ref.at[0], wu_ref.at[0], wd_ref.at[0], rows)
            if not first:
                part = acc_ref[rows, :] + part
            if final:
                o_ref[rows, :] = part.astype(o_ref.dtype)
            else:
                acc_ref[rows, :] = part

    valid = tv_ref[i] > 0
    pl.when(valid & (j == 0))(functools.partial(step, True, False))
    if n_ff > 2:
        pl.when(valid & (j > 0) & (j < last))(functools.partial(step, False, False))
    pl.when(valid & (j == last))(functools.partial(step, False, True))


def _moe_experts(x_sorted, tile_expert, tile_valid, wg, wu, wd, tm, tf):
    n_rows, d = x_sorted.shape
    dff = wg.shape[2]
    ff_tile = lambda i, j, tv: j * tv[i] + (dff // tf - 1) * (1 - tv[i])
    grid_spec = pltpu.PrefetchScalarGridSpec(
        num_scalar_prefetch=2,
        grid=(n_rows // tm, dff // tf),
        in_specs=[pl.BlockSpec((tm, d), lambda i, j, te, tv: (i, 0)),
                  pl.BlockSpec((1, d, tf), lambda i, j, te, tv: (te[i], 0, ff_tile(i, j, tv))),
                  pl.BlockSpec((1, d, tf), lambda i, j, te, tv: (te[i], 0, ff_tile(i, j, tv))),
                  pl.BlockSpec((1, tf, d), lambda i, j, te, tv: (te[i], ff_tile(i, j, tv), 0))],
        out_specs=pl.BlockSpec((tm, d), lambda i, j, te, tv: (i, 0)),
        scratch_shapes=[pltpu.VMEM((tm, d), F32)],
    )
    assert dff // tf >= 2
    return pl.pallas_call(
        functools.partial(_moe_kernel, dff // tf),
        grid_spec=grid_spec,
        out_shape=jax.ShapeDtypeStruct((n_rows, d), BF16),
        compiler_params=_params("parallel", "arbitrary"),
        name="moe_expert_swiglu",
    )(tile_expert, tile_valid, x_sorted, wg, wu, wd)


def _route_plan(route, counts, tm, n_batch, lp):
    lead = FRONT_PAD + N_META
    seq_route = route.reshape(ROUTE_FIELDS, n_batch, lp)[:ROW_R1 + 1, :, lead:]
    seq_route = seq_route.reshape(ROW_R1 + 1, -1).astype(jnp.int32)
    top_idx = seq_route[ROW_E0:ROW_E1 + 1]
    rank = seq_route[ROW_R0:ROW_R1 + 1]
    n_assign = top_idx.size
    n_rows = n_assign + N_EXPERTS * tm
    counts = counts[:N_EXPERTS, 0].astype(jnp.int32)
    padded = ((counts + tm - 1) // tm) * tm
    ends = jnp.cumsum(padded)
    starts = ends - padded
    pos = starts.at[top_idx].get(mode="promise_in_bounds") + rank
    token = (jnp.arange(n_batch, dtype=jnp.int32)[:, None] * lp + lead
             + jnp.arange(lp - lead, dtype=jnp.int32)[None, :]).reshape(-1)
    src = jnp.arange(n_rows, dtype=jnp.int32) % (n_batch * lp)
    src = src.at[pos.reshape(-1)].set(jnp.tile(token, 2), unique_indices=True,
                                      mode="promise_in_bounds")
    tile_start = jnp.arange(n_rows // tm, dtype=jnp.int32) * tm
    tile_expert = jnp.sum((tile_start[:, None] >= ends[None, :]).astype(jnp.int32), axis=1)
    tile_expert = jnp.minimum(tile_expert, N_EXPERTS - 1)
    tile_valid = (tile_start < ends[-1]).astype(jnp.int32)
    return pos, src, tile_expert, tile_valid


def _combine_ln_kernel(h_ref, y1_ref, y2_ref, r_ref, sel_ref, g_ref, b_ref, o_ref):
    wts = None
    rem = r_ref[...]
    for _ in range(3):
        piece = rem.astype(BF16)
        term = lax.dot_general(piece, sel_ref[...], TN, preferred_element_type=F32)
        wts = term if wts is None else wts + term
        rem = rem - piece.astype(F32)
    reps = h_ref.shape[1] // 128
    w0 = jnp.concatenate([wts[:, :128]] * reps, axis=1)
    w1 = jnp.concatenate([wts[:, 128:]] * reps, axis=1)
    acc = ALPHA * h_ref[...] + w0 * y1_ref[...].astype(F32) + w1 * y2_ref[...].astype(F32)
    o_ref[...] = _layer_norm(acc, g_ref[...], b_ref[...], LN_EPS)


def _combine_ln(h, y1, y2, route, g, b, n_batch, lp, tm):
    d = h.shape[1]
    seq = lp - FRONT_PAD - N_META
    per_seq = seq // tm
    row = lambda v: v.reshape(1, -1)
    first_row = lambda b, j: pl.multiple_of(b * lp + FRONT_PAD + N_META + j * tm, ATT_BLOCK)
    h_tile = pl.BlockSpec((pl.Element(tm), pl.Element(d)), lambda b, j: (first_row(b, j), 0))
    route_tile = pl.BlockSpec((pl.Element(ROUTE_FIELDS), pl.Element(tm)),
                              lambda b, j: (0, first_row(b, j)))
    tile = pl.BlockSpec((tm, d), lambda b, j: (b * per_seq + j, 0))
    vec = pl.BlockSpec((1, d), lambda b, j: (0, 0))
    sel = np.zeros((ROUTE_FIELDS, 256), np.float32)
    sel[ROW_W0, :128] = 1.0
    sel[ROW_W1, 128:] = 1.0
    return pl.pallas_call(
        _combine_ln_kernel,
        grid=(n_batch, per_seq),
        in_specs=[h_tile, tile, tile, route_tile, pl.BlockSpec(sel.shape, lambda b, j: (0, 0)),
                  vec, vec],
        out_specs=tile,
        out_shape=jax.ShapeDtypeStruct((n_batch * seq, d), F32),
        compiler_params=_params("parallel", "parallel"),
        name="moe_combine_ln",
    )(h, y1, y2, route, jnp.asarray(sel, BF16), row(g), row(b))


def _pick_tile(n, target, mult=16):
    best = mult
    for cand in range(mult, min(n, target) + 1, mult):
        if n % cand == 0:
            best = cand
    return best


def kernel(x, meta_tokens, ev_w_in, ev_conv_w, ev_conv_b, ev_convnorm_g, ev_convnorm_b, ev_shift_mu, ev_w0, ev_w2, ev_a0, ev_a2, ev_g2, ev_k_k, ev_k_a, ev_r_k, ev_lnx_g, ev_lnx_b, ev_w_out, ev_ln1_g, ev_ln1_b, ev_ffn_gate, ev_ffn_up, ev_ffn_down, ev_ln2_g, ev_ln2_b, od_w_qkv, od_b_qkv, od_sinks, od_w_o, od_b_o, od_ln1_g, od_ln1_b, od_router, od_exp_gate, od_exp_up, od_exp_down, od_ln2_g, od_ln2_b):
    n_batch, seq, d = x.shape
    lp = FRONT_PAD + N_META + seq
    assert d == D_MODEL and lp % ATT_BLOCK == 0 and lp % (CHUNK * RWKV_CHUNKS_PER_STEP) == 0
    t = n_batch * lp
    tm = _pick_tile(t, 1024)
    tm_seq = _pick_tile(lp, 1056)
    assert t % CONV_ROWS == 0

    meta = jnp.broadcast_to(meta_tokens[None].astype(x.dtype), (n_batch, N_META, d))
    h = jnp.concatenate([jnp.zeros((n_batch, FRONT_PAD, d), x.dtype), meta, x], axis=1)
    h = h.reshape(t, d)

    w_in = ev_w_in[0].astype(BF16)
    a_out = _conv_module(h, w_in[:, :2 * CONV_CH], ev_conv_w[0], ev_conv_b[0], ev_convnorm_g[0],
                         ev_convnorm_b[0])
    b_out = _rwkv_time_mix(h, w_in[:, 2 * CONV_CH:], n_batch, ev_shift_mu[0], ev_w0[0], ev_w2[0],
                           ev_a0[0], ev_a2[0], ev_g2[0], ev_k_k[0], ev_k_a[0], ev_r_k[0],
                           ev_lnx_g[0], ev_lnx_b[0])
    w_out = ev_w_out[0].astype(BF16)
    h = _proj_ln(h, [a_out, b_out], [w_out[:CONV_CH], w_out[CONV_CH:]], None,
                 ev_ln1_g[0], ev_ln1_b[0], tm)
    h = _ffn_ln(h, ev_ffn_gate[0].astype(BF16), ev_ffn_up[0].astype(BF16),
                ev_ffn_down[0].astype(BF16), ev_ln2_g[0], ev_ln2_b[0],
                _pick_tile(t, 512, SWIGLU_ROWS))

    w_qkv, b_qkv = _qkv_weights(od_w_qkv[0], od_b_qkv[0])
    q, kv = _qkv_rope(h, w_qkv.astype(BF16), b_qkv, lp, tm_seq)
    att = _attention(q, kv, od_sinks[0], n_batch, lp)
    h, h_bf16, route, counts = _oproj_ln_route(h, att, od_w_o[0].astype(BF16), od_b_o[0], od_ln1_g[0],
                                               od_ln1_b[0], od_router[0], lp,
                                               _pick_tile(t, min(1024, lp - 1), ROUTE_ROWS))
    tm_moe = _pick_tile(2 * n_batch * seq, 1024, SWIGLU_ROWS)
    pos, src, tile_expert, tile_valid = _route_plan(route, counts, tm_moe, n_batch, lp)
    x_sorted = h_bf16.at[src].get(mode="promise_in_bounds")
    exp_w = _cast_expert_weights([od_exp_gate[0], od_exp_up[0], od_exp_down[0]])
    y_sorted = _moe_experts(x_sorted, tile_expert, tile_valid, *exp_w, tm_moe, 1792)
    y1 = y_sorted.at[pos[0]].get(mode="promise_in_bounds", unique_indices=True)
    y2 = y_sorted.at[pos[1]].get(mode="promise_in_bounds", unique_indices=True)
    out = _combine_ln(h, y1, y2, route, od_ln2_g[0], od_ln2_b[0], n_batch, lp, _pick_tile(seq, 1024))
    return out.reshape(n_batch, seq, d)
```

```python
import functools
import math

import jax
import jax.numpy as jnp
import numpy as np
from jax import lax
from jax.experimental import pallas as pl
from jax.experimental.pallas import tpu as pltpu

F32 = jnp.float32
BF16 = jnp.bfloat16

D_MODEL = 1024
N_META = 16
ATT_BLOCK = 128
FRONT_PAD = (-N_META) % ATT_BLOCK
CONV_CH = 512
CONV_WIDTH = 31
RWKV_CH = 512
RWKV_HEAD = 64
RWKV_COLS = 3 * RWKV_CH + 64 + 64 + 128
HEAD_DIM = 64
N_Q_HEADS = 16
N_KV_HEADS = 2
GROUP = N_Q_HEADS // N_KV_HEADS
ROPE_THETA = 10000.0
N_EXPERTS = 8
DEPTH = 2
ALPHA = (2 * DEPTH) ** 0.25
LN_EPS = 1e-5
LNX_EPS = 64e-5

CHUNK = 64
RWKV_CHUNKS_PER_STEP = 3
HEADS_PER_GROUP = 2
GROUP_W = HEADS_PER_GROUP * RWKV_HEAD
VMEM_LIMIT = 56 * 1024 * 1024

NN = (((1,), (0,)), ((), ()))
NT = (((1,), (1,)), ((), ()))
TN = (((0,), (0,)), ((), ()))


def _params(*sem):
    return pltpu.CompilerParams(dimension_semantics=sem, vmem_limit_bytes=VMEM_LIMIT)


def _sigmoid(x):
    return 1.0 / (1.0 + jnp.exp(-x))


def _split2(x):
    hi = x.astype(BF16)
    lo = (x - hi.astype(F32)).astype(BF16)
    return hi, lo


def _dot(a, b, dims=NN, passes=1):
    if passes == 1:
        return lax.dot_general(a.astype(BF16), b.astype(BF16), dims, preferred_element_type=F32)
    ah, al = _split2(a)
    bh, bl = _split2(b)
    out = lax.dot_general(ah, bh, dims, preferred_element_type=F32)
    out = out + lax.dot_general(ah, bl, dims, preferred_element_type=F32)
    return out + lax.dot_general(al, bh, dims, preferred_element_type=F32)


def _dot_exact_rhs(a, b_bf16, dims=NN, parts=2):
    out = None
    rem = a
    for _ in range(parts):
        piece = rem.astype(BF16)
        term = lax.dot_general(piece, b_bf16, dims, preferred_element_type=F32)
        out = term if out is None else out + term
        rem = rem - piece.astype(F32)
    return out


def _layer_norm(x, g, b, eps):
    mu = jnp.mean(x, axis=-1, keepdims=True)
    xc = x - mu
    var = jnp.mean(xc * xc, axis=-1, keepdims=True)
    return xc * lax.rsqrt(var + eps) * g + b


CONV_ROWS = 256
CONV_SUB = 32
CONV_HALO = 32
SUBLANES = 8


def _conv_kernel(h_ref, w_ref, cw_ref, cb_ref, g_ref, b_ref, o_ref, buf_ref, sh_ref):
    @pl.when(pl.program_id(0) == 0)
    def _():
        buf_ref[0:CONV_HALO, :] = jnp.zeros((CONV_HALO, CONV_CH), F32)

    u = jnp.dot(h_ref[...].astype(BF16), w_ref[...], preferred_element_type=F32)
    buf_ref[CONV_HALO:CONV_HALO + CONV_ROWS, :] = u[:, :CONV_CH] * _sigmoid(u[:, CONV_CH:])
    n_sh = CONV_HALO + CONV_ROWS - SUBLANES
    for r in range(1, SUBLANES):
        sh_ref[r - 1] = buf_ref[r:r + n_sh, :]
    cw = cw_ref[...]
    base = CONV_HALO - (CONV_WIDTH - 1)
    for s in range(CONV_ROWS // CONV_SUB):
        r0 = s * CONV_SUB
        acc = jnp.broadcast_to(cb_ref[...], (CONV_SUB, CONV_CH))
        for j in range(CONV_WIDTH):
            shift = (base + j) % SUBLANES
            start = base + j - shift + r0
            if shift == 0:
                win = buf_ref[start:start + CONV_SUB, :]
            else:
                win = sh_ref[shift - 1, start:start + CONV_SUB, :]
            acc = acc + cw[j:j + 1, :] * win
        y = _layer_norm(acc, g_ref[...], b_ref[...], LN_EPS)
        o_ref[r0:r0 + CONV_SUB, :] = (y * _sigmoid(y)).astype(o_ref.dtype)
    buf_ref[0:CONV_HALO, :] = buf_ref[CONV_ROWS:CONV_ROWS + CONV_HALO, :]


def _conv_module(h, w_in_conv, conv_w, conv_b, norm_g, norm_b):
    t, d = h.shape
    row = lambda v: v.reshape(1, -1)
    full = lambda shape: pl.BlockSpec(shape, lambda i: (0, 0))
    return pl.pallas_call(
        _conv_kernel,
        grid=(t // CONV_ROWS,),
        in_specs=[pl.BlockSpec((CONV_ROWS, d), lambda i: (i, 0)), full((d, 2 * CONV_CH)),
                  full((CONV_WIDTH, CONV_CH)), full((1, CONV_CH)), full((1, CONV_CH)),
                  full((1, CONV_CH))],
        out_specs=pl.BlockSpec((CONV_ROWS, CONV_CH), lambda i: (i, 0)),
        out_shape=jax.ShapeDtypeStruct((t, CONV_CH), BF16),
        scratch_shapes=[pltpu.VMEM((CONV_ROWS + CONV_HALO, CONV_CH), F32),
                        pltpu.VMEM((SUBLANES - 1, CONV_ROWS + CONV_HALO - SUBLANES, CONV_CH), F32)],
        compiler_params=_params("arbitrary"),
        name="conv_module",
    )(h, w_in_conv, conv_w, row(conv_b), row(norm_g), row(norm_b))


def _rwkv_masks(step_rows):
    n = GROUP_W
    r = np.arange(n)[:, None]
    c = np.arange(n)[None, :]
    same = lambda w: (r // w) == (c // w)
    strict = (r > c) & same(CHUNK)
    incl = (r >= c) & same(CHUNK)
    m16 = (r > c) & same(16)
    m32 = (r > c) & same(32) & ~same(16)
    m64 = (r > c) & same(64) & ~same(32)
    stack = (c // RWKV_HEAD) == (r // CHUNK)
    masks = np.stack([strict, incl, m16, m32, m64, stack]).astype(np.float32)
    tr = np.arange(step_rows)
    tri = ((tr[:, None] >= tr[None, :]) & ((tr[:, None] // CHUNK) == (tr[None, :] // CHUNK)))
    tri = tri.astype(np.float32)
    hr = np.arange(RWKV_CH)
    head_ones = ((hr[:, None] // RWKV_HEAD) == (hr[None, :] // RWKV_HEAD)).astype(np.float32)
    return masks, tri, head_ones


def _each(f, *cols):
    return [f(*xs) for xs in zip(*cols)]


def _unit_lower_inverse_minus_identity(labs, m16, m32, m64, passes):
    mm = functools.partial(_dot, passes=passes)
    d = [lab * m16 for lab in labs]
    d2 = _each(mm, d, d)
    d4 = _each(mm, d2, d2)
    dd2 = _each(mm, d, d2)
    d8 = _each(mm, d4, d4)
    n = _each(lambda x, x2, xx2: x2 - x - xx2, d, d2, dd2)
    n = _each(lambda x, y, xy: x + y + xy, n, d4, _each(mm, n, d4))
    n = _each(lambda x, y, xy: x + y + xy, n, d8, _each(mm, n, d8))
    for mask in (m32, m64):
        off = [lab * mask for lab in labs]
        w = _each(lambda o, no: o + no, off, _each(mm, n, off))
        w = _each(lambda x, xn: x + xn, w, _each(mm, w, n))
        n = _each(lambda x, y: x - y, n, w)
    return n


def _rwkv_kernel(h_ref, win_ref, mu_ref, w0_ref, w2_ref, a0_ref, a2_ref, g2_ref, kk_ref, ka_ref,
                 rk_ref, lg_ref, lb_ref, masks_ref, tri_ref, ones_ref, o_ref, s_ref, carry_ref):
    c = CHUNK
    rows = h_ref.shape[0]
    n_groups = RWKV_CH // GROUP_W
    w = GROUP_W

    @pl.when(pl.program_id(1) == 0)
    def _():
        s_ref[...] = jnp.zeros_like(s_ref)
        carry_ref[...] = jnp.zeros_like(carry_ref)

    z = jnp.dot(h_ref[...].astype(BF16), win_ref[...], preferred_element_type=F32)
    prev = pltpu.roll(z, 1, 0)
    row_id = lax.broadcasted_iota(jnp.int32, z.shape, 0)
    prev = jnp.where(row_id == 0, jnp.broadcast_to(carry_ref[7:8, :], z.shape), prev)
    carry_ref[...] = z[rows - 8:rows, :]
    z = z + (prev - z) * mu_ref[...]

    r = z[:, 0:RWKV_CH]
    k = z[:, RWKV_CH:2 * RWKV_CH]
    v = z[:, 2 * RWKV_CH:3 * RWKV_CH]
    wa_lo = z[:, 3 * RWKV_CH:3 * RWKV_CH + 128]
    g_lo = z[:, 3 * RWKV_CH + 128:3 * RWKV_CH + 256]

    wpre = w0_ref[...] + _dot(jnp.tanh(wa_lo), w2_ref[...], passes=3)
    ew = _sigmoid(wpre) * math.exp(-0.5)
    a = _sigmoid(a0_ref[...] + _dot(wa_lo, a2_ref[...]))
    gate = _dot(_sigmoid(g_lo), g2_ref[...])

    ones = ones_ref[...]
    kk = k * kk_ref[...]
    ss = _dot_exact_rhs(kk * kk, ones, parts=1)
    kk = kk * lax.rsqrt(jnp.maximum(ss, 1e-24))
    k2 = k * (1.0 + (a - 1.0) * ka_ref[...])
    kka = kk * a

    e0 = ew.astype(BF16)
    e1f = ew - e0.astype(F32)
    e1 = e1f.astype(BF16)
    e2 = (e1f - e1.astype(F32)).astype(BF16)
    tri = tri_ref[...]
    cum = (jnp.dot(tri, e0, preferred_element_type=F32) + jnp.dot(tri, e1, preferred_element_type=F32)
           + jnp.dot(tri, e2, preferred_element_type=F32))
    gam = jnp.exp(-cum)
    igam = jnp.exp(cum)
    r_t = r * gam
    ka_t = kk * jnp.exp(ew - cum)
    b_t = kka * igam
    k_t = k2 * igam

    strict, incl, m16, m32, m64, stack = (masks_ref[i] for i in range(6))
    p1 = 1
    mm = functools.partial(_dot, passes=p1)

    units = [(q, g) for q in range(rows // c) for g in range(n_groups)]

    def stacked(x):
        return [jnp.concatenate([x[q * c:(q + 1) * c, g * w:(g + 1) * w]] * HEADS_PER_GROUP, axis=0)
                * stack for q, g in units]

    xa, xr, xv, yb, yk = stacked(ka_t), stacked(r_t), stacked(v), stacked(b_t), stacked(k_t)
    xar = _each(lambda top, bot: jnp.concatenate([top, bot], axis=0), xa, xr)
    ybk = _each(lambda top, bot: jnp.concatenate([top, bot], axis=0), yb, yk)
    pbk = _each(lambda x, y: mm(x, y, NT), xar, ybk)
    lab = [p[:w, :w] * strict for p in pbk]
    arb = [p[w:, :w] * incl for p in pbk]
    lak = [p[:w, w:] * strict for p in pbk]
    ark = [p[w:, w:] * incl for p in pbk]
    n_inv = _unit_lower_inverse_minus_identity(lab, m16, m32, m64, p1)
    lv = _each(mm, lak, xv)
    rhs = _each(lambda left, right: jnp.concatenate([left, right], axis=1), xa, lv)
    gz = _each(lambda x, nx: x + nx, rhs, _each(mm, n_inv, rhs))
    corr = _each(mm, arb, gz)
    qh = _each(lambda x, cr: x - cr[:, :w], xr, corr)
    yloc = _each(lambda av, cr: av - cr[:, w:], _each(mm, ark, xv), corr)
    gzb = _each(lambda x, y: mm(x, y, TN), gz, yb)
    vk = _each(lambda x, y: mm(x, y, TN), xv, yk)

    state = [s_ref[g] for g in range(n_groups)]
    y_rows = []
    for q in range(rows // c):
        gam_end = gam[(q + 1) * c - 1:(q + 1) * c, :]
        ys = []
        for g in range(n_groups):
            i = q * n_groups + g
            s0 = state[g]
            y_st = mm(qh[i], s0, NT) + yloc[i]
            y_heads = y_st[0:c]
            for h in range(1, HEADS_PER_GROUP):
                y_heads = y_heads + y_st[h * c:(h + 1) * c]
            ys.append(y_heads)
            s_new = s0 - mm(s0, gzb[i][:w]) + vk[i] - gzb[i][w:]
            state[g] = s_new * gam_end[:, g * w:(g + 1) * w]
        y_rows.append(jnp.concatenate(ys, axis=1))
    for g in range(n_groups):
        s_ref[g] = state[g]
    y = jnp.concatenate(y_rows, axis=0)

    inv_n = 1.0 / RWKV_HEAD
    mean = _dot_exact_rhs(y, ones) * inv_n
    yc = y - mean
    var = _dot_exact_rhs(yc * yc, ones, parts=1) * inv_n
    yn = yc * lax.rsqrt(var + LNX_EPS) * lg_ref[...] + lb_ref[...]
    bonus = _dot_exact_rhs(r * k2 * rk_ref[...], ones, parts=1) * v
    o_ref[...] = ((yn + bonus) * gate).astype(o_ref.dtype)


def _rwkv_time_mix(h, w_in_rwkv, n_batch, shift_mu, w0, w2, a0, a2, g2, k_k, k_a, r_k, lnx_g, lnx_b):
    t, d = h.shape
    step_rows = CHUNK * RWKV_CHUNKS_PER_STEP
    n_steps = t // n_batch // step_rows
    masks, tri, head_ones = _rwkv_masks(step_rows)
    row = lambda v: v.reshape(1, -1).astype(F32)
    zeros64 = jnp.zeros((64, RWKV_CH), F32)
    w2p = jnp.concatenate([w2, zeros64], axis=0)
    a2p = jnp.concatenate([zeros64, a2], axis=0)
    full2 = lambda shape: pl.BlockSpec(shape, lambda b, c: (0, 0))
    vec = full2((1, RWKV_CH))
    return pl.pallas_call(
        _rwkv_kernel,
        grid=(n_batch, n_steps),
        in_specs=[pl.BlockSpec((step_rows, d), lambda b, c: (b * n_steps + c, 0)),
                  full2((d, RWKV_COLS)), full2((1, RWKV_COLS)), vec, full2((128, RWKV_CH)), vec, full2((128, RWKV_CH)),
                  full2((128, RWKV_CH)), vec, vec, vec, vec, vec,
                  pl.BlockSpec(masks.shape, lambda b, c: (0, 0, 0)),
                  full2((step_rows, step_rows)), full2((RWKV_CH, RWKV_CH))],
        out_specs=pl.BlockSpec((step_rows, RWKV_CH), lambda b, c: (b * n_steps + c, 0)),
        out_shape=jax.ShapeDtypeStruct((t, RWKV_CH), BF16),
        scratch_shapes=[pltpu.VMEM((RWKV_CH // GROUP_W, GROUP_W, GROUP_W), F32),
                        pltpu.VMEM((8, RWKV_COLS), F32)],
        compiler_params=_params("arbitrary", "arbitrary"),
        name="rwkv7_time_mix",
    )(h, w_in_rwkv, row(shift_mu), row(w0), w2p, row(a0), a2p, g2, row(k_k), row(k_a), row(r_k),
      row(lnx_g), row(lnx_b), jnp.asarray(masks), jnp.asarray(tri, BF16),
      jnp.asarray(head_ones, BF16))


def _proj_ln_kernel(n_in, has_bias, *refs):
    h_ref = refs[0]
    xs = refs[1:1 + n_in]
    ws = refs[1 + n_in:1 + 2 * n_in]
    rest = refs[1 + 2 * n_in:]
    if has_bias:
        bias_ref, g_ref, b_ref, o_ref = rest
    else:
        g_ref, b_ref, o_ref = rest
    acc = ALPHA * h_ref[...]
    for x_ref, w_ref in zip(xs, ws):
        acc = acc + jnp.dot(x_ref[...].astype(BF16), w_ref[...], preferred_element_type=F32)
    if has_bias:
        acc = acc + bias_ref[...]
    o_ref[...] = _layer_norm(acc, g_ref[...], b_ref[...], LN_EPS)


def _proj_ln(h, xs, ws, bias, g, b, tm):
    t, d = h.shape
    row = lambda v: v.reshape(1, -1)
    full = lambda shape: pl.BlockSpec(shape, lambda i: (0, 0))
    in_specs = [pl.BlockSpec((tm, d), lambda i: (i, 0))]
    in_specs += [pl.BlockSpec((tm, x.shape[1]), lambda i: (i, 0)) for x in xs]
    in_specs += [full(w.shape) for w in ws]
    args = [h, *xs, *ws]
    if bias is not None:
        in_specs.append(full((1, d)))
        args.append(row(bias))
    in_specs += [full((1, d)), full((1, d))]
    args += [row(g), row(b)]
    return pl.pallas_call(
        functools.partial(_proj_ln_kernel, len(xs), bias is not None),
        grid=(t // tm,),
        in_specs=in_specs,
        out_specs=pl.BlockSpec((tm, d), lambda i: (i, 0)),
        out_shape=jax.ShapeDtypeStruct((t, d), F32),
        compiler_params=_params("parallel"),
        name="proj_residual_ln",
    )(*args)


SWIGLU_ROWS = 256


def _swiglu_rows(x_ref, wg_ref, wu_ref, wd_ref, rows):
    xb = x_ref[rows, :].astype(BF16)
    gate = jnp.dot(xb, wg_ref[...], preferred_element_type=F32)
    up = jnp.dot(xb, wu_ref[...], preferred_element_type=F32)
    act = (gate * _sigmoid(gate) * up).astype(BF16)
    return jnp.dot(act, wd_ref[...], preferred_element_type=F32)


def _row_blocks(n_rows):
    assert n_rows % SWIGLU_ROWS == 0
    return [slice(r0, r0 + SWIGLU_ROWS) for r0 in range(0, n_rows, SWIGLU_ROWS)]


def _ffn_ln_kernel(h_ref, wg_ref, wu_ref, wd_ref, g_ref, b_ref, o_ref):
    for rows in _row_blocks(h_ref.shape[0]):
        acc = ALPHA * h_ref[rows, :] + _swiglu_rows(h_ref, wg_ref, wu_ref, wd_ref, rows)
        o_ref[rows, :] = _layer_norm(acc, g_ref[...], b_ref[...], LN_EPS)


def _ffn_ln(h, wg, wu, wd, g, b, tm):
    t, d = h.shape
    dff = wg.shape[1]
    row = lambda v: v.reshape(1, -1)
    full = lambda shape: pl.BlockSpec(shape, lambda i: (0, 0))
    return pl.pallas_call(
        _ffn_ln_kernel,
        grid=(t // tm,),
        in_specs=[pl.BlockSpec((tm, d), lambda i: (i, 0)), full((d, dff)), full((d, dff)),
                  full((dff, d)), full((1, d)), full((1, d))],
        out_specs=pl.BlockSpec((tm, d), lambda i: (i, 0)),
        out_shape=jax.ShapeDtypeStruct((t, d), F32),
        compiler_params=_params("parallel"),
        name="swiglu_residual_ln",
    )(h, wg, wu, wd, row(g), row(b))


def _cast_kernel(*refs):
    n = len(refs) // 2
    for src, dst in zip(refs[:n], refs[n:]):
        dst[...] = src[...].astype(dst.dtype)


CAST_STEPS = 32


def _cast_expert_weights(ws):
    flat = [w.reshape(-1, w.shape[-1]) for w in ws]
    assert all(f.shape[0] % (8 * CAST_STEPS) == 0 for f in flat)
    specs = [pl.BlockSpec((f.shape[0] // CAST_STEPS, f.shape[1]), lambda i: (i, 0)) for f in flat]
    outs = pl.pallas_call(
        _cast_kernel,
        grid=(CAST_STEPS,),
        in_specs=specs,
        out_specs=specs,
        out_shape=[jax.ShapeDtypeStruct(f.shape, BF16) for f in flat],
        compiler_params=_params("parallel"),
        name="cast_bf16",
    )(*flat)
    return [o.reshape(w.shape) for o, w in zip(outs, ws)]


Q_COLS = N_Q_HEADS * HEAD_DIM
K_COLS = 2 * N_KV_HEADS * 128
V_COLS = N_KV_HEADS * 128
KV_COLS = K_COLS + V_COLS


def _qkv_rope_kernel(h_ref, w_ref, bias_ref, cos_ref, sin_ref, q_ref, kv_ref):
    acc = jnp.dot(h_ref[...].astype(BF16), w_ref[...], preferred_element_type=F32) + bias_ref[...]
    cos = cos_ref[...]
    sin = sin_ref[...]
    lane = lax.broadcasted_iota(jnp.int32, cos.shape, 1)
    first_half = (lane % HEAD_DIM) < (HEAD_DIM // 2)
    for c0 in range(0, Q_COLS + K_COLS, 128):
        x = acc[:, c0:c0 + 128]
        swapped = jnp.where(first_half, pltpu.roll(x, 128 - HEAD_DIM // 2, 1),
                            pltpu.roll(x, HEAD_DIM // 2, 1))
        y = x * cos + swapped * sin
        if c0 < Q_COLS:
            q_ref[:, c0:c0 + 128] = (y * (HEAD_DIM ** -0.5)).astype(BF16)
        else:
            kv_ref[:, c0 - Q_COLS:c0 - Q_COLS + 128] = y.astype(BF16)
    kv_ref[:, K_COLS:] = acc[:, Q_COLS + K_COLS:].astype(BF16)


def _rope_tables(lp):
    half = HEAD_DIM // 2
    inv = ROPE_THETA ** (-jnp.arange(half, dtype=F32) / half)
    pos = (jnp.arange(lp) - FRONT_PAD).astype(F32)
    ang = pos[:, None] * inv[None, :]
    cos = jnp.cos(ang)
    sin = jnp.sin(ang)
    cos = jnp.concatenate([cos, cos] * (128 // HEAD_DIM), axis=1)
    sin = jnp.concatenate([-sin, sin] * (128 // HEAD_DIM), axis=1)
    return cos, sin


def _qkv_rope(h, w, bias, lp, tm):
    t, d = h.shape
    n = w.shape[1]
    cos, sin = _rope_tables(lp)
    per_seq = lp // tm
    return pl.pallas_call(
        _qkv_rope_kernel,
        grid=(t // tm,),
        in_specs=[pl.BlockSpec((tm, d), lambda i: (i, 0)),
                  pl.BlockSpec((d, n), lambda i: (0, 0)),
                  pl.BlockSpec((1, n), lambda i: (0, 0)),
                  pl.BlockSpec((tm, 128), lambda i: (i % per_seq, 0)),
                  pl.BlockSpec((tm, 128), lambda i: (i % per_seq, 0))],
        out_specs=[pl.BlockSpec((tm, Q_COLS), lambda i: (i, 0)),
                   pl.BlockSpec((tm, KV_COLS), lambda i: (i, 0))],
        out_shape=[jax.ShapeDtypeStruct((t, Q_COLS), BF16), jax.ShapeDtypeStruct((t, KV_COLS), BF16)],
        compiler_params=_params("parallel"),
        name="qkv_rope",
    )(h, w, bias.reshape(1, -1), cos, sin)


def _qkv_weights(w_qkv, b_qkv):
    kv_w = N_KV_HEADS * HEAD_DIM
    wq, wk, wv = jnp.split(w_qkv, [Q_COLS, Q_COLS + kv_w], axis=1)
    bq, bk, bv = jnp.split(b_qkv[None], [Q_COLS, Q_COLS + kv_w], axis=1)

    def expand(m):
        zero = jnp.zeros((m.shape[0], HEAD_DIM), m.dtype)
        k_cols, v_cols = [], []
        for g in range(N_KV_HEADS):
            col = m[:, g * HEAD_DIM:(g + 1) * HEAD_DIM]
            k_cols += [col, zero, zero, col]
            v_cols += [col, col]
        return k_cols, v_cols

    w_cols = [wq] + expand(wk)[0] + expand(wv)[1]
    b_cols = [bq] + expand(bk)[0] + expand(bv)[1]
    return jnp.concatenate(w_cols, axis=1), jnp.concatenate(b_cols, axis=1)[0]


NEG = -1e30


def _attn_kernel(sink_ref, q_ref, kv_prev_ref, kv_cur_ref, o_ref):
    n = pl.program_id(1)
    blk = ATT_BLOCK
    kv = jnp.concatenate([kv_prev_ref[...], kv_cur_ref[...]], axis=0)
    qi = lax.broadcasted_iota(jnp.int32, (blk, 2 * blk), 0)
    kj = lax.broadcasted_iota(jnp.int32, (blk, 2 * blk), 1)
    first_key = FRONT_PAD - (n - 1) * blk
    valid = (kj > qi) & (kj <= qi + blk) & (kj >= first_key)
    low = lax.broadcasted_iota(jnp.int32, (blk, 128), 1) < HEAD_DIM
    for pair in range(N_Q_HEADS // 2):
        grp = (2 * pair) // GROUP
        vd = kv[:, K_COLS + grp * 128:K_COLS + (grp + 1) * 128]
        qp = q_ref[:, pair * 128:(pair + 1) * 128]
        outs = []
        for half in range(2):
            head = 2 * pair + half
            kh = kv[:, (2 * grp + half) * 128:(2 * grp + half + 1) * 128]
            s = lax.dot_general(qp, kh, NT, preferred_element_type=F32)
            s = jnp.where(valid, s, NEG)
            sink = sink_ref[head]
            m = jnp.maximum(jnp.max(s, axis=-1, keepdims=True), sink)
            p = jnp.exp(s - m)
            denom = jnp.sum(p, axis=-1, keepdims=True) + jnp.exp(sink - m)
            o = jnp.dot(p.astype(BF16), vd, preferred_element_type=F32)
            outs.append(o * (1.0 / denom))
        o_ref[:, pair * 128:(pair + 1) * 128] = jnp.where(low, outs[0], outs[1]).astype(o_ref.dtype)


def _attention(q, kv, sinks, n_batch, lp):
    t = q.shape[0]
    nb = lp // ATT_BLOCK
    return pl.pallas_call(
        _attn_kernel,
        grid=(n_batch, nb),
        in_specs=[pl.BlockSpec(memory_space=pltpu.SMEM),
                  pl.BlockSpec((ATT_BLOCK, Q_COLS), lambda b, n: (b * nb + n, 0)),
                  pl.BlockSpec((ATT_BLOCK, KV_COLS), lambda b, n: (b * nb + jnp.maximum(n - 1, 0), 0)),
                  pl.BlockSpec((ATT_BLOCK, KV_COLS), lambda b, n: (b * nb + n, 0))],
        out_specs=pl.BlockSpec((ATT_BLOCK, Q_COLS), lambda b, n: (b * nb + n, 0)),
        out_shape=jax.ShapeDtypeStruct((t, Q_COLS), BF16),
        compiler_params=_params("parallel", "parallel"),
        name="swa_sink_attention",
    )(sinks.astype(F32), q, kv, kv)


ROW_E0, ROW_E1, ROW_W0, ROW_W1, ROW_R0, ROW_R1 = range(6)
ROUTE_FIELDS = 16
ROUTE_ROWS = 256


def _oproj_ln_route_kernel(lp, h_ref, x_ref, w_ref, bias_ref, g_ref, b_ref, rwt_ref, tri_ref,
                           o_ref, ob_ref, route_ref, count_ref, run_ref):
    @pl.when(pl.program_id(0) == 0)
    def _():
        run_ref[...] = jnp.zeros_like(run_ref)

    tm = h_ref.shape[0]
    for r0 in range(0, tm, ROUTE_ROWS):
        rows = slice(r0, r0 + ROUTE_ROWS)
        acc = ALPHA * h_ref[rows, :] + bias_ref[...] + jnp.dot(
            x_ref[rows, :], w_ref[...], preferred_element_type=F32)
        hn = _layer_norm(acc, g_ref[...], b_ref[...], LN_EPS)
        o_ref[rows, :] = hn
        ob_ref[rows, :] = hn.astype(BF16)

    sub = lax.broadcasted_iota(jnp.int32, (ROUTE_FIELDS, tm), 0)
    lane = lax.broadcasted_iota(jnp.int32, (ROUTE_FIELDS, tm), 1)
    logits = _dot(rwt_ref[...], o_ref[...], NT, passes=3)
    logits = jnp.where(sub < N_EXPERTS, logits, -jnp.inf)
    v1 = jnp.max(logits, axis=0, keepdims=True)
    i1 = jnp.min(jnp.where(logits == v1, sub, ROUTE_FIELDS), axis=0, keepdims=True)
    rest = jnp.where(sub == i1, -jnp.inf, logits)
    v2 = jnp.max(rest, axis=0, keepdims=True)
    i2 = jnp.min(jnp.where(rest == v2, sub, ROUTE_FIELDS), axis=0, keepdims=True)
    w1 = 1.0 / (1.0 + jnp.exp(v2 - v1))
    w2 = 1.0 - w1

    first = pl.program_id(0) * tm
    lead = ((first + tm - 1) // lp) * lp
    row = first + lane
    routed = (row < lead) | (row >= lead + FRONT_PAD + N_META)

    run = run_ref[:, 0:1]
    oh1 = jnp.where(routed & (sub == i1), 1.0, 0.0)
    oh2 = jnp.where(routed & (sub == i2), 1.0, 0.0)
    both = oh1 + oh2
    before = jnp.dot(both.astype(BF16), tri_ref[...], preferred_element_type=F32) + run
    r1 = jnp.sum(oh1 * before, axis=0, keepdims=True)
    r2 = jnp.sum(oh2 * before, axis=0, keepdims=True)
    run = run + jnp.sum(both, axis=1, keepdims=True)

    out = jnp.where(sub == ROW_E0, i1.astype(F32), 0.0)
    out = jnp.where(sub == ROW_E1, i2.astype(F32), out)
    out = jnp.where(sub == ROW_W0, w1, out)
    out = jnp.where(sub == ROW_W1, w2, out)
    out = jnp.where(sub == ROW_R0, r1, out)
    out = jnp.where(sub == ROW_R1, r2, out)
    route_ref[...] = out
    run_ref[...] = jnp.broadcast_to(run, run_ref.shape)
    count_ref[...] = jnp.broadcast_to(run, count_ref.shape)


def _oproj_ln_route(h, x, w, bias, g, b, router_w, lp, tm):
    t, d = h.shape
    assert tm % ROUTE_ROWS == 0 and lp > tm
    row = lambda v: v.reshape(1, -1)
    rwt = jnp.zeros((ROUTE_FIELDS, d), F32).at[:N_EXPERTS].set(router_w.T)
    tri = jnp.asarray(np.triu(np.ones((tm, tm), np.float32), 1), BF16)
    tile = lambda width: pl.BlockSpec((tm, width), lambda i: (i, 0))
    full = lambda shape: pl.BlockSpec(shape, lambda i: (0, 0))
    return pl.pallas_call(
        functools.partial(_oproj_ln_route_kernel, lp),
        grid=(t // tm,),
        in_specs=[tile(d), tile(x.shape[1]), full(w.shape), full((1, d)), full((1, d)), full((1, d)),
                  full((ROUTE_FIELDS, d)), full((tm, tm))],
        out_specs=[tile(d), tile(d), pl.BlockSpec((ROUTE_FIELDS, tm), lambda i: (0, i)),
                   full((ROUTE_FIELDS, 128))],
        out_shape=[jax.ShapeDtypeStruct((t, d), F32), jax.ShapeDtypeStruct((t, d), BF16),
                   jax.ShapeDtypeStruct((ROUTE_FIELDS, t), F32),
                   jax.ShapeDtypeStruct((ROUTE_FIELDS, 128), F32)],
        scratch_shapes=[pltpu.VMEM((ROUTE_FIELDS, 128), F32)],
        compiler_params=_params("arbitrary"),
        name="oproj_ln_route",
    )(h, x, w, row(bias), row(g), row(b), rwt, tri)


def _moe_kernel(n_ff, te_ref, tv_ref, x_ref, wg_ref, wu_ref, wd_ref, o_ref, acc_ref):
    i = pl.program_id(0)
    j = pl.program_id(1)
    last = n_ff - 1

    def step(first, final):
        for rows in _row_blocks(x_ref.shape[0]):
            part = _swiglu_rows(x_ref, wg_ref.at[0], wu_ref.at[0], wd_ref.at[0], rows)
            if not first:
                part = acc_ref[rows, :] + part
            if final:
                o_ref[rows, :] = part.astype(o_ref.dtype)
            else:
                acc_ref[rows, :] = part

    valid = tv_ref[i] > 0
    pl.when(valid & (j == 0))(functools.partial(step, True, False))
    if n_ff > 2:
        pl.when(valid & (j > 0) & (j < last))(functools.partial(step, False, False))
    pl.when(valid & (j == last))(functools.partial(step, False, True))

    @pl.when(jnp.logical_not(valid) & (j == last))
    def _():
        o_ref[...] = jnp.zeros_like(o_ref)


def _moe_experts(x_sorted, tile_expert, tile_valid, wg, wu, wd, tm, tf):
    n_rows, d = x_sorted.shape
    dff = wg.shape[2]
    ff_tile = lambda i, j, tv: j * tv[i] + (dff // tf - 1) * (1 - tv[i])
    grid_spec = pltpu.PrefetchScalarGridSpec(
        num_scalar_prefetch=2,
        grid=(n_rows // tm, dff // tf),
        in_specs=[pl.BlockSpec((tm, d), lambda i, j, te, tv: (i, 0)),
                  pl.BlockSpec((1, d, tf), lambda i, j, te, tv: (te[i], 0, ff_tile(i, j, tv))),
                  pl.BlockSpec((1, d, tf), lambda i, j, te, tv: (te[i], 0, ff_tile(i, j, tv))),
                  pl.BlockSpec((1, tf, d), lambda i, j, te, tv: (te[i], ff_tile(i, j, tv), 0))],
        out_specs=pl.BlockSpec((tm, d), lambda i, j, te, tv: (i, 0)),
        scratch_shapes=[pltpu.VMEM((tm, d), F32)],
    )
    assert dff // tf >= 2
    return pl.pallas_call(
        functools.partial(_moe_kernel, dff // tf),
        grid_spec=grid_spec,
        out_shape=jax.ShapeDtypeStruct((n_rows, d), BF16),
        compiler_params=_params("parallel", "arbitrary"),
        name="moe_expert_swiglu",
    )(tile_expert, tile_valid, x_sorted, wg, wu, wd)


def _route_plan(route, counts, tm, n_batch, lp):
    lead = FRONT_PAD + N_META
    seq_route = route.reshape(ROUTE_FIELDS, n_batch, lp)[:ROW_R1 + 1, :, lead:]
    seq_route = seq_route.reshape(ROW_R1 + 1, -1).astype(jnp.int32)
    top_idx = seq_route[ROW_E0:ROW_E1 + 1]
    rank = seq_route[ROW_R0:ROW_R1 + 1]
    n_assign = top_idx.size
    n_rows = n_assign + N_EXPERTS * tm
    counts = counts[:N_EXPERTS, 0].astype(jnp.int32)
    padded = ((counts + tm - 1) // tm) * tm
    ends = jnp.cumsum(padded)
    starts = ends - padded
    start_of = jnp.zeros_like(top_idx)
    for e in range(N_EXPERTS):
        start_of = jnp.where(top_idx == e, starts[e], start_of)
    pos = start_of + rank
    token = (jnp.arange(n_batch, dtype=jnp.int32)[:, None] * lp + lead
             + jnp.arange(lp - lead, dtype=jnp.int32)[None, :]).reshape(-1)
    src = jnp.arange(n_rows, dtype=jnp.int32) % (n_batch * lp)
    src = src.at[pos.reshape(-1)].set(jnp.tile(token, 2), unique_indices=True,
                                      mode="promise_in_bounds")
    tile_start = jnp.arange(n_rows // tm, dtype=jnp.int32) * tm
    tile_expert = jnp.sum((tile_start[:, None] >= ends[None, :]).astype(jnp.int32), axis=1)
    tile_expert = jnp.minimum(tile_expert, N_EXPERTS - 1)
    tile_valid = (tile_start < ends[-1]).astype(jnp.int32)
    return pos, src, tile_expert, tile_valid


def _combine_ln_kernel(h_ref, y1_ref, y2_ref, r_ref, sel_ref, g_ref, b_ref, o_ref):
    wts = None
    rem = r_ref[...]
    for _ in range(3):
        piece = rem.astype(BF16)
        term = lax.dot_general(piece, sel_ref[...], TN, preferred_element_type=F32)
        wts = term if wts is None else wts + term
        rem = rem - piece.astype(F32)
    reps = h_ref.shape[1] // 128
    w0 = jnp.concatenate([wts[:, :128]] * reps, axis=1)
    w1 = jnp.concatenate([wts[:, 128:]] * reps, axis=1)
    acc = ALPHA * h_ref[...] + w0 * y1_ref[...].astype(F32) + w1 * y2_ref[...].astype(F32)
    o_ref[...] = _layer_norm(acc, g_ref[...], b_ref[...], LN_EPS)


def _combine_ln(h, y1, y2, route, g, b, n_batch, lp, tm):
    d = h.shape[1]
    seq = lp - FRONT_PAD - N_META
    per_seq = seq // tm
    row = lambda v: v.reshape(1, -1)
    first_row = lambda b, j: pl.multiple_of(b * lp + FRONT_PAD + N_META + j * tm, ATT_BLOCK)
    h_tile = pl.BlockSpec((pl.Element(tm), pl.Element(d)), lambda b, j: (first_row(b, j), 0))
    route_tile = pl.BlockSpec((pl.Element(ROUTE_FIELDS), pl.Element(tm)),
                              lambda b, j: (0, first_row(b, j)))
    tile = pl.BlockSpec((tm, d), lambda b, j: (b * per_seq + j, 0))
    vec = pl.BlockSpec((1, d), lambda b, j: (0, 0))
    sel = np.zeros((ROUTE_FIELDS, 256), np.float32)
    sel[ROW_W0, :128] = 1.0
    sel[ROW_W1, 128:] = 1.0
    return pl.pallas_call(
        _combine_ln_kernel,
        grid=(n_batch, per_seq),
        in_specs=[h_tile, tile, tile, route_tile, pl.BlockSpec(sel.shape, lambda b, j: (0, 0)),
                  vec, vec],
        out_specs=tile,
        out_shape=jax.ShapeDtypeStruct((n_batch * seq, d), F32),
        compiler_params=_params("parallel", "parallel"),
        name="moe_combine_ln",
    )(h, y1, y2, route, jnp.asarray(sel, BF16), row(g), row(b))


def _pick_tile(n, target, mult=16):
    best = mult
    for cand in range(mult, min(n, target) + 1, mult):
        if n % cand == 0:
            best = cand
    return best


def kernel(x, meta_tokens, ev_w_in, ev_conv_w, ev_conv_b, ev_convnorm_g, ev_convnorm_b, ev_shift_mu, ev_w0, ev_w2, ev_a0, ev_a2, ev_g2, ev_k_k, ev_k_a, ev_r_k, ev_lnx_g, ev_lnx_b, ev_w_out, ev_ln1_g, ev_ln1_b, ev_ffn_gate, ev_ffn_up, ev_ffn_down, ev_ln2_g, ev_ln2_b, od_w_qkv, od_b_qkv, od_sinks, od_w_o, od_b_o, od_ln1_g, od_ln1_b, od_router, od_exp_gate, od_exp_up, od_exp_down, od_ln2_g, od_ln2_b):
    n_batch, seq, d = x.shape
    lp = FRONT_PAD + N_META + seq
    assert d == D_MODEL and lp % ATT_BLOCK == 0 and lp % (CHUNK * RWKV_CHUNKS_PER_STEP) == 0
    t = n_batch * lp
    tm = _pick_tile(t, 1024)
    tm_seq = _pick_tile(lp, 1056)
    assert t % CONV_ROWS == 0

    meta = jnp.broadcast_to(meta_tokens[None].astype(x.dtype), (n_batch, N_META, d))
    h = jnp.concatenate([jnp.zeros((n_batch, FRONT_PAD, d), x.dtype), meta, x], axis=1)
    h = h.reshape(t, d)

    w_in = ev_w_in[0].astype(BF16)
    a_out = _conv_module(h, w_in[:, :2 * CONV_CH], ev_conv_w[0], ev_conv_b[0], ev_convnorm_g[0],
                         ev_convnorm_b[0])
    b_out = _rwkv_time_mix(h, w_in[:, 2 * CONV_CH:], n_batch, ev_shift_mu[0], ev_w0[0], ev_w2[0],
                           ev_a0[0], ev_a2[0], ev_g2[0], ev_k_k[0], ev_k_a[0], ev_r_k[0],
                           ev_lnx_g[0], ev_lnx_b[0])
    w_out = ev_w_out[0].astype(BF16)
    h = _proj_ln(h, [a_out, b_out], [w_out[:CONV_CH], w_out[CONV_CH:]], None,
                 ev_ln1_g[0], ev_ln1_b[0], tm)
    h = _ffn_ln(h, ev_ffn_gate[0].astype(BF16), ev_ffn_up[0].astype(BF16),
                ev_ffn_down[0].astype(BF16), ev_ln2_g[0], ev_ln2_b[0],
                _pick_tile(t, 512, SWIGLU_ROWS))

    w_qkv, b_qkv = _qkv_weights(od_w_qkv[0], od_b_qkv[0])
    q, kv = _qkv_rope(h, w_qkv.astype(BF16), b_qkv, lp, tm_seq)
    att = _attention(q, kv, od_sinks[0], n_batch, lp)
    h, h_bf16, route, counts = _oproj_ln_route(h, att, od_w_o[0].astype(BF16), od_b_o[0], od_ln1_g[0],
                                               od_ln1_b[0], od_router[0], lp,
                                               _pick_tile(t, min(1024, lp - 1), ROUTE_ROWS))
    tm_moe = _pick_tile(2 * n_batch * seq, 1024, SWIGLU_ROWS)
    pos, src, tile_expert, tile_valid = _route_plan(route, counts, tm_moe, n_batch, lp)
    x_sorted = h_bf16.at[src].get(mode="promise_in_bounds")
    exp_w = _cast_expert_weights([od_exp_gate[0], od_exp_up[0], od_exp_down[0]])
    y_sorted = _moe_experts(x_sorted, tile_expert, tile_valid, *exp_w, tm_moe, 1792)
    y1 = y_sorted.at[pos[0]].get(mode="promise_in_bounds", unique_indices=True)
    y2 = y_sorted.at[pos[1]].get(mode="promise_in_bounds", unique_indices=True)
    out = _combine_ln(h, y1, y2, route, od_ln2_g[0], od_ln2_b[0], n_batch, lp, _pick_tile(seq, 1024))
    return out.reshape(n_batch, seq, d)
```

```python
import functools
import math

import jax
import jax.numpy as jnp
import numpy as np
from jax import lax
from jax.experimental import pallas as pl
from jax.experimental.pallas import tpu as pltpu

F32 = jnp.float32
BF16 = jnp.bfloat16

D_MODEL = 1024
N_META = 16
ATT_BLOCK = 128
FRONT_PAD = (-N_META) % ATT_BLOCK
CONV_CH = 512
CONV_WIDTH = 31
RWKV_CH = 512
RWKV_HEAD = 64
RWKV_COLS = 3 * RWKV_CH + 64 + 64 + 128
HEAD_DIM = 64
N_Q_HEADS = 16
N_KV_HEADS = 2
GROUP = N_Q_HEADS // N_KV_HEADS
ROPE_THETA = 10000.0
N_EXPERTS = 8
DEPTH = 2
ALPHA = (2 * DEPTH) ** 0.25
LN_EPS = 1e-5
LNX_EPS = 64e-5

CHUNK = 64
RWKV_CHUNKS_PER_STEP = 3
HEADS_PER_GROUP = 2
GROUP_W = HEADS_PER_GROUP * RWKV_HEAD
VMEM_LIMIT = 56 * 1024 * 1024

NN = (((1,), (0,)), ((), ()))
NT = (((1,), (1,)), ((), ()))
TN = (((0,), (0,)), ((), ()))


def _params(*sem):
    return pltpu.CompilerParams(dimension_semantics=sem, vmem_limit_bytes=VMEM_LIMIT)


def _sigmoid(x):
    return 1.0 / (1.0 + jnp.exp(-x))


def _split2(x):
    hi = x.astype(BF16)
    lo = (x - hi.astype(F32)).astype(BF16)
    return hi, lo


def _dot(a, b, dims=NN, passes=1):
    if passes == 1:
        return lax.dot_general(a.astype(BF16), b.astype(BF16), dims, preferred_element_type=F32)
    ah, al = _split2(a)
    bh, bl = _split2(b)
    out = lax.dot_general(ah, bh, dims, preferred_element_type=F32)
    out = out + lax.dot_general(ah, bl, dims, preferred_element_type=F32)
    return out + lax.dot_general(al, bh, dims, preferred_element_type=F32)


def _dot_exact_rhs(a, b_bf16, dims=NN, parts=2):
    out = None
    rem = a
    for _ in range(parts):
        piece = rem.astype(BF16)
        term = lax.dot_general(piece, b_bf16, dims, preferred_element_type=F32)
        out = term if out is None else out + term
        rem = rem - piece.astype(F32)
    return out


def _layer_norm(x, g, b, eps):
    mu = jnp.mean(x, axis=-1, keepdims=True)
    xc = x - mu
    var = jnp.mean(xc * xc, axis=-1, keepdims=True)
    return xc * lax.rsqrt(var + eps) * g + b


CONV_ROWS = 512
CONV_SUB = 32
CONV_HALO = 32
SUBLANES = 8


def _conv_kernel(h_ref, w_ref, cw_ref, cb_ref, g_ref, b_ref, o_ref, buf_ref, sh_ref):
    @pl.when(pl.program_id(0) == 0)
    def _():
        buf_ref[0:CONV_HALO, :] = jnp.zeros((CONV_HALO, CONV_CH), F32)

    n_rows = h_ref.shape[0]
    u = jnp.dot(h_ref[...].astype(BF16), w_ref[...], preferred_element_type=F32)
    buf_ref[CONV_HALO:CONV_HALO + n_rows, :] = u[:, :CONV_CH] * _sigmoid(u[:, CONV_CH:])
    n_sh = CONV_HALO + n_rows - SUBLANES
    for r in range(1, SUBLANES):
        sh_ref[r - 1] = buf_ref[r:r + n_sh, :]
    cw = cw_ref[...]
    base = CONV_HALO - (CONV_WIDTH - 1)
    for s in range(n_rows // CONV_SUB):
        r0 = s * CONV_SUB
        acc = jnp.broadcast_to(cb_ref[...], (CONV_SUB, CONV_CH))
        for j in range(CONV_WIDTH):
            shift = (base + j) % SUBLANES
            start = base + j - shift + r0
            if shift == 0:
                win = buf_ref[start:start + CONV_SUB, :]
            else:
                win = sh_ref[shift - 1, start:start + CONV_SUB, :]
            acc = acc + cw[j:j + 1, :] * win
        y = _layer_norm(acc, g_ref[...], b_ref[...], LN_EPS)
        o_ref[r0:r0 + CONV_SUB, :] = (y * _sigmoid(y)).astype(o_ref.dtype)
    buf_ref[0:CONV_HALO, :] = buf_ref[n_rows:n_rows + CONV_HALO, :]


def _conv_module(h, w_in_conv, conv_w, conv_b, norm_g, norm_b):
    t, d = h.shape
    n_rows = _pick_tile(t, CONV_ROWS, CONV_SUB)
    row = lambda v: v.reshape(1, -1)
    full = lambda shape: pl.BlockSpec(shape, lambda i: (0, 0))
    return pl.pallas_call(
        _conv_kernel,
        grid=(t // n_rows,),
        in_specs=[pl.BlockSpec((n_rows, d), lambda i: (i, 0)), full((d, 2 * CONV_CH)),
                  full((CONV_WIDTH, CONV_CH)), full((1, CONV_CH)), full((1, CONV_CH)),
                  full((1, CONV_CH))],
        out_specs=pl.BlockSpec((n_rows, CONV_CH), lambda i: (i, 0)),
        out_shape=jax.ShapeDtypeStruct((t, CONV_CH), BF16),
        scratch_shapes=[pltpu.VMEM((n_rows + CONV_HALO, CONV_CH), F32),
                        pltpu.VMEM((SUBLANES - 1, n_rows + CONV_HALO - SUBLANES, CONV_CH), F32)],
        compiler_params=_params("arbitrary"),
        name="conv_module",
    )(h, w_in_conv, conv_w, row(conv_b), row(norm_g), row(norm_b))


def _rwkv_masks(step_rows):
    n = GROUP_W
    r = np.arange(n)[:, None]
    c = np.arange(n)[None, :]
    same = lambda w: (r // w) == (c // w)
    strict = (r > c) & same(CHUNK)
    incl = (r >= c) & same(CHUNK)
    m16 = (r > c) & same(16)
    m32 = (r > c) & same(32) & ~same(16)
    m64 = (r > c) & same(64) & ~same(32)
    stack = (c // RWKV_HEAD) == (r // CHUNK)
    masks = np.stack([strict, incl, m16, m32, m64, stack]).astype(np.float32)
    tr = np.arange(step_rows)
    tri = ((tr[:, None] >= tr[None, :]) & ((tr[:, None] // CHUNK) == (tr[None, :] // CHUNK)))
    tri = tri.astype(np.float32)
    hr = np.arange(RWKV_CH)
    head_ones = ((hr[:, None] // RWKV_HEAD) == (hr[None, :] // RWKV_HEAD)).astype(np.float32)
    return masks, tri, head_ones


def _each(f, *cols):
    return [f(*xs) for xs in zip(*cols)]


def _unit_lower_inverse_minus_identity(labs, m16, m32, m64, passes):
    mm = functools.partial(_dot, passes=passes)
    d = [lab * m16 for lab in labs]
    d2 = _each(mm, d, d)
    d4 = _each(mm, d2, d2)
    dd2 = _each(mm, d, d2)
    d8 = _each(mm, d4, d4)
    n = _each(lambda x, x2, xx2: x2 - x - xx2, d, d2, dd2)
    n = _each(lambda x, y, xy: x + y + xy, n, d4, _each(mm, n, d4))
    n = _each(lambda x, y, xy: x + y + xy, n, d8, _each(mm, n, d8))
    for mask in (m32, m64):
        off = [lab * mask for lab in labs]
        w = _each(lambda o, no: o + no, off, _each(mm, n, off))
        w = _each(lambda x, xn: x + xn, w, _each(mm, w, n))
        n = _each(lambda x, y: x - y, n, w)
    return n


def _rwkv_kernel(h_ref, win_ref, mu_ref, w0_ref, w2_ref, a0_ref, a2_ref, g2_ref, kk_ref, ka_ref,
                 rk_ref, lg_ref, lb_ref, masks_ref, tri_ref, ones_ref, o_ref, s_ref, carry_ref):
    c = CHUNK
    rows = h_ref.shape[0]
    n_groups = RWKV_CH // GROUP_W
    w = GROUP_W

    @pl.when(pl.program_id(1) == 0)
    def _():
        s_ref[...] = jnp.zeros_like(s_ref)
        carry_ref[...] = jnp.zeros_like(carry_ref)

    z = jnp.dot(h_ref[...].astype(BF16), win_ref[...], preferred_element_type=F32)
    prev = pltpu.roll(z, 1, 0)
    row_id = lax.broadcasted_iota(jnp.int32, z.shape, 0)
    prev = jnp.where(row_id == 0, jnp.broadcast_to(carry_ref[7:8, :], z.shape), prev)
    carry_ref[...] = z[rows - 8:rows, :]
    z = z + (prev - z) * mu_ref[...]

    r = z[:, 0:RWKV_CH]
    k = z[:, RWKV_CH:2 * RWKV_CH]
    v = z[:, 2 * RWKV_CH:3 * RWKV_CH]
    wa_lo = z[:, 3 * RWKV_CH:3 * RWKV_CH + 128]
    g_lo = z[:, 3 * RWKV_CH + 128:3 * RWKV_CH + 256]

    wpre = w0_ref[...] + _dot(jnp.tanh(wa_lo), w2_ref[...], passes=3)
    ew = _sigmoid(wpre) * math.exp(-0.5)
    a = _sigmoid(a0_ref[...] + _dot(wa_lo, a2_ref[...]))
    gate = _dot(_sigmoid(g_lo), g2_ref[...])

    ones = ones_ref[...]
    kk = k * kk_ref[...]
    ss = _dot_exact_rhs(kk * kk, ones, parts=1)
    kk = kk * lax.rsqrt(jnp.maximum(ss, 1e-24))
    k2 = k * (1.0 + (a - 1.0) * ka_ref[...])
    kka = kk * a

    e0 = ew.astype(BF16)
    e1f = ew - e0.astype(F32)
    e1 = e1f.astype(BF16)
    e2 = (e1f - e1.astype(F32)).astype(BF16)
    tri = tri_ref[...]
    cum = (jnp.dot(tri, e0, preferred_element_type=F32) + jnp.dot(tri, e1, preferred_element_type=F32)
           + jnp.dot(tri, e2, preferred_element_type=F32))
    gam = jnp.exp(-cum)
    igam = jnp.exp(cum)
    r_t = r * gam
    ka_t = kk * jnp.exp(ew - cum)
    b_t = kka * igam
    k_t = k2 * igam

    strict, incl, m16, m32, m64, stack = (masks_ref[i] for i in range(6))
    p1 = 1
    mm = functools.partial(_dot, passes=p1)

    units = [(q, g) for q in range(rows // c) for g in range(n_groups)]

    def stacked(x):
        return [jnp.concatenate([x[q * c:(q + 1) * c, g * w:(g + 1) * w]] * HEADS_PER_GROUP, axis=0)
                * stack for q, g in units]

    xa, xr, xv, yb, yk = stacked(ka_t), stacked(r_t), stacked(v), stacked(b_t), stacked(k_t)
    xar = _each(lambda top, bot: jnp.concatenate([top, bot], axis=0), xa, xr)
    ybk = _each(lambda top, bot: jnp.concatenate([top, bot], axis=0), yb, yk)
    pbk = _each(lambda x, y: mm(x, y, NT), xar, ybk)
    lab = [p[:w, :w] * strict for p in pbk]
    arb = [p[w:, :w] * incl for p in pbk]
    lak = [p[:w, w:] * strict for p in pbk]
    ark = [p[w:, w:] * incl for p in pbk]
    n_inv = _unit_lower_inverse_minus_identity(lab, m16, m32, m64, p1)
    lv = _each(mm, lak, xv)
    rhs = _each(lambda left, right: jnp.concatenate([left, right], axis=1), xa, lv)
    gz = _each(lambda x, nx: x + nx, rhs, _each(mm, n_inv, rhs))
    corr = _each(mm, arb, gz)
    qh = _each(lambda x, cr: x - cr[:, :w], xr, corr)
    yloc = _each(lambda av, cr: av - cr[:, w:], _each(mm, ark, xv), corr)
    gzb = _each(lambda x, y: mm(x, y, TN), gz, yb)
    vk = _each(lambda x, y: mm(x, y, TN), xv, yk)

    state = [s_ref[g] for g in range(n_groups)]
    y_rows = []
    for q in range(rows // c):
        gam_end = gam[(q + 1) * c - 1:(q + 1) * c, :]
        ys = []
        for g in range(n_groups):
            i = q * n_groups + g
            s0 = state[g]
            y_st = mm(qh[i], s0, NT) + yloc[i]
            y_heads = y_st[0:c]
            for h in range(1, HEADS_PER_GROUP):
                y_heads = y_heads + y_st[h * c:(h + 1) * c]
            ys.append(y_heads)
            s_new = s0 - mm(s0, gzb[i][:w]) + vk[i] - gzb[i][w:]
            state[g] = s_new * gam_end[:, g * w:(g + 1) * w]
        y_rows.append(jnp.concatenate(ys, axis=1))
    for g in range(n_groups):
        s_ref[g] = state[g]
    y = jnp.concatenate(y_rows, axis=0)

    inv_n = 1.0 / RWKV_HEAD
    mean = _dot_exact_rhs(y, ones) * inv_n
    yc = y - mean
    var = _dot_exact_rhs(yc * yc, ones, parts=1) * inv_n
    yn = yc * lax.rsqrt(var + LNX_EPS) * lg_ref[...] + lb_ref[...]
    bonus = _dot_exact_rhs(r * k2 * rk_ref[...], ones, parts=1) * v
    o_ref[...] = ((yn + bonus) * gate).astype(o_ref.dtype)


def _rwkv_time_mix(h, w_in_rwkv, n_batch, shift_mu, w0, w2, a0, a2, g2, k_k, k_a, r_k, lnx_g, lnx_b):
    t, d = h.shape
    step_rows = CHUNK * RWKV_CHUNKS_PER_STEP
    n_steps = t // n_batch // step_rows
    masks, tri, head_ones = _rwkv_masks(step_rows)
    row = lambda v: v.reshape(1, -1).astype(F32)
    zeros64 = jnp.zeros((64, RWKV_CH), F32)
    w2p = jnp.concatenate([w2, zeros64], axis=0)
    a2p = jnp.concatenate([zeros64, a2], axis=0)
    full2 = lambda shape: pl.BlockSpec(shape, lambda b, c: (0, 0))
    vec = full2((1, RWKV_CH))
    return pl.pallas_call(
        _rwkv_kernel,
        grid=(n_batch, n_steps),
        in_specs=[pl.BlockSpec((step_rows, d), lambda b, c: (b * n_steps + c, 0)),
                  full2((d, RWKV_COLS)), full2((1, RWKV_COLS)), vec, full2((128, RWKV_CH)), vec, full2((128, RWKV_CH)),
                  full2((128, RWKV_CH)), vec, vec, vec, vec, vec,
                  pl.BlockSpec(masks.shape, lambda b, c: (0, 0, 0)),
                  full2((step_rows, step_rows)), full2((RWKV_CH, RWKV_CH))],
        out_specs=pl.BlockSpec((step_rows, RWKV_CH), lambda b, c: (b * n_steps + c, 0)),
        out_shape=jax.ShapeDtypeStruct((t, RWKV_CH), BF16),
        scratch_shapes=[pltpu.VMEM((RWKV_CH // GROUP_W, GROUP_W, GROUP_W), F32),
                        pltpu.VMEM((8, RWKV_COLS), F32)],
        compiler_params=_params("arbitrary", "arbitrary"),
        name="rwkv7_time_mix",
    )(h, w_in_rwkv, row(shift_mu), row(w0), w2p, row(a0), a2p, g2, row(k_k), row(k_a), row(r_k),
      row(lnx_g), row(lnx_b), jnp.asarray(masks), jnp.asarray(tri, BF16),
      jnp.asarray(head_ones, BF16))


SWIGLU_ROWS = 256


def _swiglu_rows(x_ref, wg_ref, wu_ref, wd_ref, rows):
    xb = x_ref[rows, :].astype(BF16)
    gate = jnp.dot(xb, wg_ref[...], preferred_element_type=F32)
    up = jnp.dot(xb, wu_ref[...], preferred_element_type=F32)
    act = (gate * _sigmoid(gate) * up).astype(BF16)
    return jnp.dot(act, wd_ref[...], preferred_element_type=F32)


def _row_blocks(n_rows):
    assert n_rows % SWIGLU_ROWS == 0
    return [slice(r0, r0 + SWIGLU_ROWS) for r0 in range(0, n_rows, SWIGLU_ROWS)]


def _mix_ffn_kernel(h_ref, a_ref, b_ref, wa_ref, wb_ref, g1_ref, b1_ref, wg_ref, wu_ref, wd_ref,
                    g2_ref, b2_ref, o_ref, h1_ref):
    for rows in _row_blocks(h_ref.shape[0]):
        mix = (jnp.dot(a_ref[rows, :], wa_ref[...], preferred_element_type=F32)
               + jnp.dot(b_ref[rows, :], wb_ref[...], preferred_element_type=F32))
        h1 = _layer_norm(ALPHA * h_ref[rows, :] + mix, g1_ref[...], b1_ref[...], LN_EPS)
        h1_ref[rows, :] = h1
        acc = ALPHA * h1 + _swiglu_rows(h1_ref, wg_ref, wu_ref, wd_ref, rows)
        o_ref[rows, :] = _layer_norm(acc, g2_ref[...], b2_ref[...], LN_EPS)


def _mix_ffn(h, a_out, b_out, w_a, w_b, g1, b1, wg, wu, wd, g2, b2, tm):
    t, d = h.shape
    dff = wg.shape[1]
    row = lambda v: v.reshape(1, -1)
    full = lambda shape: pl.BlockSpec(shape, lambda i: (0, 0))
    tile = lambda width: pl.BlockSpec((tm, width), lambda i: (i, 0))
    return pl.pallas_call(
        _mix_ffn_kernel,
        grid=(t // tm,),
        in_specs=[tile(d), tile(a_out.shape[1]), tile(b_out.shape[1]), full(w_a.shape), full(w_b.shape),
                  full((1, d)), full((1, d)), full((d, dff)), full((d, dff)), full((dff, d)),
                  full((1, d)), full((1, d))],
        out_specs=tile(d),
        out_shape=jax.ShapeDtypeStruct((t, d), F32),
        scratch_shapes=[pltpu.VMEM((tm, d), F32)],
        compiler_params=_params("parallel"),
        name="mix_ffn_ln",
    )(h, a_out, b_out, w_a, w_b, row(g1), row(b1), wg, wu, wd, row(g2), row(b2))


def _cast_kernel(*refs):
    n = len(refs) // 2
    for src, dst in zip(refs[:n], refs[n:]):
        dst[...] = src[...].astype(dst.dtype)


CAST_STEPS = 32


def _cast_expert_weights(ws):
    flat = [w.reshape(-1, w.shape[-1]) for w in ws]
    assert all(f.shape[0] % (8 * CAST_STEPS) == 0 for f in flat)
    specs = [pl.BlockSpec((f.shape[0] // CAST_STEPS, f.shape[1]), lambda i: (i, 0)) for f in flat]
    outs = pl.pallas_call(
        _cast_kernel,
        grid=(CAST_STEPS,),
        in_specs=specs,
        out_specs=specs,
        out_shape=[jax.ShapeDtypeStruct(f.shape, BF16) for f in flat],
        compiler_params=_params("parallel"),
        name="cast_bf16",
    )(*flat)
    return [o.reshape(w.shape) for o, w in zip(outs, ws)]


Q_COLS = N_Q_HEADS * HEAD_DIM
K_COLS = 2 * N_KV_HEADS * 128
V_COLS = N_KV_HEADS * 128
KV_COLS = K_COLS + V_COLS


def _qkv_rope_kernel(h_ref, w_ref, bias_ref, cos_ref, sin_ref, q_ref, kv_ref):
    acc = jnp.dot(h_ref[...].astype(BF16), w_ref[...], preferred_element_type=F32) + bias_ref[...]
    cos = cos_ref[...]
    sin = sin_ref[...]
    lane = lax.broadcasted_iota(jnp.int32, cos.shape, 1)
    first_half = (lane % HEAD_DIM) < (HEAD_DIM // 2)
    for c0 in range(0, Q_COLS + K_COLS, 128):
        x = acc[:, c0:c0 + 128]
        swapped = jnp.where(first_half, pltpu.roll(x, 128 - HEAD_DIM // 2, 1),
                            pltpu.roll(x, HEAD_DIM // 2, 1))
        y = x * cos + swapped * sin
        if c0 < Q_COLS:
            q_ref[:, c0:c0 + 128] = (y * (HEAD_DIM ** -0.5)).astype(BF16)
        else:
            kv_ref[:, c0 - Q_COLS:c0 - Q_COLS + 128] = y.astype(BF16)
    kv_ref[:, K_COLS:] = acc[:, Q_COLS + K_COLS:].astype(BF16)


def _rope_tables(lp):
    half = HEAD_DIM // 2
    inv = ROPE_THETA ** (-jnp.arange(half, dtype=F32) / half)
    pos = (jnp.arange(lp) - FRONT_PAD).astype(F32)
    ang = pos[:, None] * inv[None, :]
    cos = jnp.cos(ang)
    sin = jnp.sin(ang)
    cos = jnp.concatenate([cos, cos] * (128 // HEAD_DIM), axis=1)
    sin = jnp.concatenate([-sin, sin] * (128 // HEAD_DIM), axis=1)
    return cos, sin


def _qkv_rope(h, w, bias, lp, tm):
    t, d = h.shape
    n = w.shape[1]
    cos, sin = _rope_tables(lp)
    per_seq = lp // tm
    return pl.pallas_call(
        _qkv_rope_kernel,
        grid=(t // tm,),
        in_specs=[pl.BlockSpec((tm, d), lambda i: (i, 0)),
                  pl.BlockSpec((d, n), lambda i: (0, 0)),
                  pl.BlockSpec((1, n), lambda i: (0, 0)),
                  pl.BlockSpec((tm, 128), lambda i: (i % per_seq, 0)),
                  pl.BlockSpec((tm, 128), lambda i: (i % per_seq, 0))],
        out_specs=[pl.BlockSpec((tm, Q_COLS), lambda i: (i, 0)),
                   pl.BlockSpec((tm, KV_COLS), lambda i: (i, 0))],
        out_shape=[jax.ShapeDtypeStruct((t, Q_COLS), BF16), jax.ShapeDtypeStruct((t, KV_COLS), BF16)],
        compiler_params=_params("parallel"),
        name="qkv_rope",
    )(h, w, bias.reshape(1, -1), cos, sin)


def _qkv_weights(w_qkv, b_qkv):
    kv_w = N_KV_HEADS * HEAD_DIM
    wq, wk, wv = jnp.split(w_qkv, [Q_COLS, Q_COLS + kv_w], axis=1)
    bq, bk, bv = jnp.split(b_qkv[None], [Q_COLS, Q_COLS + kv_w], axis=1)

    def expand(m):
        zero = jnp.zeros((m.shape[0], HEAD_DIM), m.dtype)
        k_cols, v_cols = [], []
        for g in range(N_KV_HEADS):
            col = m[:, g * HEAD_DIM:(g + 1) * HEAD_DIM]
            k_cols += [col, zero, zero, col]
            v_cols += [col, col]
        return k_cols, v_cols

    w_cols = [wq] + expand(wk)[0] + expand(wv)[1]
    b_cols = [bq] + expand(bk)[0] + expand(bv)[1]
    return jnp.concatenate(w_cols, axis=1), jnp.concatenate(b_cols, axis=1)[0]


NEG = -1e30


def _attn_kernel(sink_ref, q_ref, kv_prev_ref, kv_cur_ref, o_ref):
    n = pl.program_id(1)
    blk = ATT_BLOCK
    kv = jnp.concatenate([kv_prev_ref[...], kv_cur_ref[...]], axis=0)
    qi = lax.broadcasted_iota(jnp.int32, (blk, 2 * blk), 0)
    kj = lax.broadcasted_iota(jnp.int32, (blk, 2 * blk), 1)
    first_key = FRONT_PAD - (n - 1) * blk
    valid = (kj > qi) & (kj <= qi + blk) & (kj >= first_key)
    low = lax.broadcasted_iota(jnp.int32, (blk, 128), 1) < HEAD_DIM
    for pair in range(N_Q_HEADS // 2):
        grp = (2 * pair) // GROUP
        vd = kv[:, K_COLS + grp * 128:K_COLS + (grp + 1) * 128]
        qp = q_ref[:, pair * 128:(pair + 1) * 128]
        outs = []
        for half in range(2):
            head = 2 * pair + half
            kh = kv[:, (2 * grp + half) * 128:(2 * grp + half + 1) * 128]
            s = lax.dot_general(qp, kh, NT, preferred_element_type=F32)
            s = jnp.where(valid, s, NEG)
            sink = sink_ref[head]
            m = jnp.maximum(jnp.max(s, axis=-1, keepdims=True), sink)
            p = jnp.exp(s - m)
            denom = jnp.sum(p, axis=-1, keepdims=True) + jnp.exp(sink - m)
            o = jnp.dot(p.astype(BF16), vd, preferred_element_type=F32)
            outs.append(o * (1.0 / denom))
        o_ref[:, pair * 128:(pair + 1) * 128] = jnp.where(low, outs[0], outs[1]).astype(o_ref.dtype)


def _attention(q, kv, sinks, n_batch, lp):
    t = q.shape[0]
    nb = lp // ATT_BLOCK
    return pl.pallas_call(
        _attn_kernel,
        grid=(n_batch, nb),
        in_specs=[pl.BlockSpec(memory_space=pltpu.SMEM),
                  pl.BlockSpec((ATT_BLOCK, Q_COLS), lambda b, n: (b * nb + n, 0)),
                  pl.BlockSpec((ATT_BLOCK, KV_COLS), lambda b, n: (b * nb + jnp.maximum(n - 1, 0), 0)),
                  pl.BlockSpec((ATT_BLOCK, KV_COLS), lambda b, n: (b * nb + n, 0))],
        out_specs=pl.BlockSpec((ATT_BLOCK, Q_COLS), lambda b, n: (b * nb + n, 0)),
        out_shape=jax.ShapeDtypeStruct((t, Q_COLS), BF16),
        compiler_params=_params("parallel", "parallel"),
        name="swa_sink_attention",
    )(sinks.astype(F32), q, kv, kv)


ROW_E0, ROW_E1, ROW_W0, ROW_W1, ROW_R0, ROW_R1 = range(6)
ROUTE_FIELDS = 16
ROUTE_ROWS = 256


def _oproj_ln_route_kernel(lp, h_ref, x_ref, w_ref, bias_ref, g_ref, b_ref, rwt_ref, tri_ref,
                           o_ref, ob_ref, route_ref, count_ref, run_ref):
    @pl.when(pl.program_id(0) == 0)
    def _():
        run_ref[...] = jnp.zeros_like(run_ref)

    tm = h_ref.shape[0]
    for r0 in range(0, tm, ROUTE_ROWS):
        rows = slice(r0, r0 + ROUTE_ROWS)
        acc = ALPHA * h_ref[rows, :] + bias_ref[...] + jnp.dot(
            x_ref[rows, :], w_ref[...], preferred_element_type=F32)
        hn = _layer_norm(acc, g_ref[...], b_ref[...], LN_EPS)
        o_ref[rows, :] = hn
        ob_ref[rows, :] = hn.astype(BF16)

    sub = lax.broadcasted_iota(jnp.int32, (ROUTE_FIELDS, tm), 0)
    lane = lax.broadcasted_iota(jnp.int32, (ROUTE_FIELDS, tm), 1)
    logits = _dot(rwt_ref[...], o_ref[...], NT, passes=3)
    logits = jnp.where(sub < N_EXPERTS, logits, -jnp.inf)
    v1 = jnp.max(logits, axis=0, keepdims=True)
    i1 = jnp.min(jnp.where(logits == v1, sub, ROUTE_FIELDS), axis=0, keepdims=True)
    rest = jnp.where(sub == i1, -jnp.inf, logits)
    v2 = jnp.max(rest, axis=0, keepdims=True)
    i2 = jnp.min(jnp.where(rest == v2, sub, ROUTE_FIELDS), axis=0, keepdims=True)
    w1 = 1.0 / (1.0 + jnp.exp(v2 - v1))
    w2 = 1.0 - w1

    first = pl.program_id(0) * tm
    lead = ((first + tm - 1) // lp) * lp
    row = first + lane
    routed = (row < lead) | (row >= lead + FRONT_PAD + N_META)

    run = run_ref[:, 0:1]
    oh1 = jnp.where(routed & (sub == i1), 1.0, 0.0)
    oh2 = jnp.where(routed & (sub == i2), 1.0, 0.0)
    both = oh1 + oh2
    before = jnp.dot(both.astype(BF16), tri_ref[...], preferred_element_type=F32) + run
    r1 = jnp.sum(oh1 * before, axis=0, keepdims=True)
    r2 = jnp.sum(oh2 * before, axis=0, keepdims=True)
    run = run + jnp.sum(both, axis=1, keepdims=True)

    out = jnp.where(sub == ROW_E0, i1.astype(F32), 0.0)
    out = jnp.where(sub == ROW_E1, i2.astype(F32), out)
    out = jnp.where(sub == ROW_W0, w1, out)
    out = jnp.where(sub == ROW_W1, w2, out)
    out = jnp.where(sub == ROW_R0, r1, out)
    out = jnp.where(sub == ROW_R1, r2, out)
    route_ref[...] = out
    run_ref[...] = jnp.broadcast_to(run, run_ref.shape)
    count_ref[...] = jnp.broadcast_to(run, count_ref.shape)


def _oproj_ln_route(h, x, w, bias, g, b, router_w, lp, tm):
    t, d = h.shape
    assert tm % ROUTE_ROWS == 0 and lp > tm
    row = lambda v: v.reshape(1, -1)
    rwt = jnp.zeros((ROUTE_FIELDS, d), F32).at[:N_EXPERTS].set(router_w.T)
    tri = jnp.asarray(np.triu(np.ones((tm, tm), np.float32), 1), BF16)
    tile = lambda width: pl.BlockSpec((tm, width), lambda i: (i, 0))
    full = lambda shape: pl.BlockSpec(shape, lambda i: (0, 0))
    return pl.pallas_call(
        functools.partial(_oproj_ln_route_kernel, lp),
        grid=(t // tm,),
        in_specs=[tile(d), tile(x.shape[1]), full(w.shape), full((1, d)), full((1, d)), full((1, d)),
                  full((ROUTE_FIELDS, d)), full((tm, tm))],
        out_specs=[tile(d), tile(d), pl.BlockSpec((ROUTE_FIELDS, tm), lambda i: (0, i)),
                   full((ROUTE_FIELDS, 128))],
        out_shape=[jax.ShapeDtypeStruct((t, d), F32), jax.ShapeDtypeStruct((t, d), BF16),
                   jax.ShapeDtypeStruct((ROUTE_FIELDS, t), F32),
                   jax.ShapeDtypeStruct((ROUTE_FIELDS, 128), F32)],
        scratch_shapes=[pltpu.VMEM((ROUTE_FIELDS, 128), F32)],
        compiler_params=_params("arbitrary"),
        name="oproj_ln_route",
    )(h, x, w, row(bias), row(g), row(b), rwt, tri)


def _moe_kernel(n_ff, te_ref, tv_ref, x_ref, wg_ref, wu_ref, wd_ref, o_ref, acc_ref):
    i = pl.program_id(0)
    j = pl.program_id(1)
    last = n_ff - 1

    def step(first, final):
        for rows in _row_blocks(x_ref.shape[0]):
            part = _swiglu_rows(x_ref, wg_ref.at[0], wu_ref.at[0], wd_ref.at[0], rows)
            if not first:
                part = acc_ref[rows, :] + part
            if final:
                o_ref[rows, :] = part.astype(o_ref.dtype)
            else:
                acc_ref[rows, :] = part

    valid = tv_ref[i] > 0
    pl.when(valid & (j == 0))(functools.partial(step, True, False))
    if n_ff > 2:
        pl.when(valid & (j > 0) & (j < last))(functools.partial(step, False, False))
    pl.when(valid & (j == last))(functools.partial(step, False, True))

    @pl.when(jnp.logical_not(valid) & (j == last))
    def _():
        o_ref[...] = jnp.zeros_like(o_ref)


def _moe_experts(x_sorted, tile_expert, tile_valid, wg, wu, wd, tm, tf):
    n_rows, d = x_sorted.shape
    dff = wg.shape[2]
    ff_tile = lambda i, j, tv: j * tv[i] + (dff // tf - 1) * (1 - tv[i])
    grid_spec = pltpu.PrefetchScalarGridSpec(
        num_scalar_prefetch=2,
        grid=(n_rows // tm, dff // tf),
        in_specs=[pl.BlockSpec((tm, d), lambda i, j, te, tv: (i, 0)),
                  pl.BlockSpec((1, d, tf), lambda i, j, te, tv: (te[i], 0, ff_tile(i, j, tv))),
                  pl.BlockSpec((1, d, tf), lambda i, j, te, tv: (te[i], 0, ff_tile(i, j, tv))),
                  pl.BlockSpec((1, tf, d), lambda i, j, te, tv: (te[i], ff_tile(i, j, tv), 0))],
        out_specs=pl.BlockSpec((tm, d), lambda i, j, te, tv: (i, 0)),
        scratch_shapes=[pltpu.VMEM((tm, d), F32)],
    )
    assert dff // tf >= 2
    return pl.pallas_call(
        functools.partial(_moe_kernel, dff // tf),
        grid_spec=grid_spec,
        out_shape=jax.ShapeDtypeStruct((n_rows, d), BF16),
        compiler_params=_params("parallel", "arbitrary"),
        name="moe_expert_swiglu",
    )(tile_expert, tile_valid, x_sorted, wg, wu, wd)


def _route_plan(route, counts, tm, n_batch, lp):
    lead = FRONT_PAD + N_META
    seq_route = route.reshape(ROUTE_FIELDS, n_batch, lp)[:ROW_R1 + 1, :, lead:]
    seq_route = seq_route.reshape(ROW_R1 + 1, -1).astype(jnp.int32)
    top_idx = seq_route[ROW_E0:ROW_E1 + 1]
    rank = seq_route[ROW_R0:ROW_R1 + 1]
    n_assign = top_idx.size
    n_rows = n_assign + N_EXPERTS * tm
    counts = counts[:N_EXPERTS, 0].astype(jnp.int32)
    padded = ((counts + tm - 1) // tm) * tm
    ends = jnp.cumsum(padded)
    starts = ends - padded
    start_of = jnp.zeros_like(top_idx)
    for e in range(N_EXPERTS):
        start_of = jnp.where(top_idx == e, starts[e], start_of)
    pos = start_of + rank
    token = (jnp.arange(n_batch, dtype=jnp.int32)[:, None] * lp + lead
             + jnp.arange(lp - lead, dtype=jnp.int32)[None, :]).reshape(-1)
    src = jnp.arange(n_rows, dtype=jnp.int32) % (n_batch * lp)
    src = src.at[pos.reshape(-1)].set(jnp.tile(token, 2), unique_indices=True,
                                      mode="promise_in_bounds")
    tile_start = jnp.arange(n_rows // tm, dtype=jnp.int32) * tm
    tile_expert = jnp.sum((tile_start[:, None] >= ends[None, :]).astype(jnp.int32), axis=1)
    tile_expert = jnp.minimum(tile_expert, N_EXPERTS - 1)
    tile_valid = (tile_start < ends[-1]).astype(jnp.int32)
    return pos, src, tile_expert, tile_valid


def _combine_ln_kernel(h_ref, y1_ref, y2_ref, r_ref, sel_ref, g_ref, b_ref, o_ref):
    wts = None
    rem = r_ref[...]
    for _ in range(3):
        piece = rem.astype(BF16)
        term = lax.dot_general(piece, sel_ref[...], TN, preferred_element_type=F32)
        wts = term if wts is None else wts + term
        rem = rem - piece.astype(F32)
    reps = h_ref.shape[1] // 128
    w0 = jnp.concatenate([wts[:, :128]] * reps, axis=1)
    w1 = jnp.concatenate([wts[:, 128:]] * reps, axis=1)
    acc = ALPHA * h_ref[...] + w0 * y1_ref[...].astype(F32) + w1 * y2_ref[...].astype(F32)
    o_ref[...] = _layer_norm(acc, g_ref[...], b_ref[...], LN_EPS)


def _combine_ln(h, y1, y2, route, g, b, n_batch, lp, tm):
    d = h.shape[1]
    seq = lp - FRONT_PAD - N_META
    per_seq = seq // tm
    row = lambda v: v.reshape(1, -1)
    first_row = lambda b, j: pl.multiple_of(b * lp + FRONT_PAD + N_META + j * tm, ATT_BLOCK)
    h_tile = pl.BlockSpec((pl.Element(tm), pl.Element(d)), lambda b, j: (first_row(b, j), 0))
    route_tile = pl.BlockSpec((pl.Element(ROUTE_FIELDS), pl.Element(tm)),
                              lambda b, j: (0, first_row(b, j)))
    tile = pl.BlockSpec((tm, d), lambda b, j: (b * per_seq + j, 0))
    vec = pl.BlockSpec((1, d), lambda b, j: (0, 0))
    sel = np.zeros((ROUTE_FIELDS, 256), np.float32)
    sel[ROW_W0, :128] = 1.0
    sel[ROW_W1, 128:] = 1.0
    return pl.pallas_call(
        _combine_ln_kernel,
        grid=(n_batch, per_seq),
        in_specs=[h_tile, tile, tile, route_tile, pl.BlockSpec(sel.shape, lambda b, j: (0, 0)),
                  vec, vec],
        out_specs=tile,
        out_shape=jax.ShapeDtypeStruct((n_batch * seq, d), F32),
        compiler_params=_params("parallel", "parallel"),
        name="moe_combine_ln",
    )(h, y1, y2, route, jnp.asarray(sel, BF16), row(g), row(b))


def _pick_tile(n, target, mult=16):
    best = mult
    for cand in range(mult, min(n, target) + 1, mult):
        if n % cand == 0:
            best = cand
    return best


def kernel(x, meta_tokens, ev_w_in, ev_conv_w, ev_conv_b, ev_convnorm_g, ev_convnorm_b, ev_shift_mu, ev_w0, ev_w2, ev_a0, ev_a2, ev_g2, ev_k_k, ev_k_a, ev_r_k, ev_lnx_g, ev_lnx_b, ev_w_out, ev_ln1_g, ev_ln1_b, ev_ffn_gate, ev_ffn_up, ev_ffn_down, ev_ln2_g, ev_ln2_b, od_w_qkv, od_b_qkv, od_sinks, od_w_o, od_b_o, od_ln1_g, od_ln1_b, od_router, od_exp_gate, od_exp_up, od_exp_down, od_ln2_g, od_ln2_b):
    n_batch, seq, d = x.shape
    lp = FRONT_PAD + N_META + seq
    assert d == D_MODEL and lp % ATT_BLOCK == 0 and lp % (CHUNK * RWKV_CHUNKS_PER_STEP) == 0
    t = n_batch * lp
    tm = _pick_tile(t, 1024)
    tm_seq = _pick_tile(lp, 1056)

    meta = jnp.broadcast_to(meta_tokens[None].astype(x.dtype), (n_batch, N_META, d))
    h = jnp.concatenate([jnp.zeros((n_batch, FRONT_PAD, d), x.dtype), meta, x], axis=1)
    h = h.reshape(t, d)

    w_in = ev_w_in[0].astype(BF16)
    a_out = _conv_module(h, w_in[:, :2 * CONV_CH], ev_conv_w[0], ev_conv_b[0], ev_convnorm_g[0],
                         ev_convnorm_b[0])
    b_out = _rwkv_time_mix(h, w_in[:, 2 * CONV_CH:], n_batch, ev_shift_mu[0], ev_w0[0], ev_w2[0],
                           ev_a0[0], ev_a2[0], ev_g2[0], ev_k_k[0], ev_k_a[0], ev_r_k[0],
                           ev_lnx_g[0], ev_lnx_b[0])
    w_out = ev_w_out[0].astype(BF16)
    h = _mix_ffn(h, a_out, b_out, w_out[:CONV_CH], w_out[CONV_CH:], ev_ln1_g[0], ev_ln1_b[0],
                 ev_ffn_gate[0].astype(BF16), ev_ffn_up[0].astype(BF16), ev_ffn_down[0].astype(BF16),
                 ev_ln2_g[0], ev_ln2_b[0], _pick_tile(t, 512, SWIGLU_ROWS))

    w_qkv, b_qkv = _qkv_weights(od_w_qkv[0], od_b_qkv[0])
    q, kv = _qkv_rope(h, w_qkv.astype(BF16), b_qkv, lp, tm_seq)
    att = _attention(q, kv, od_sinks[0], n_batch, lp)
    h, h_bf16, route, counts = _oproj_ln_route(h, att, od_w_o[0].astype(BF16), od_b_o[0], od_ln1_g[0],
                                               od_ln1_b[0], od_router[0], lp,
                                               _pick_tile(t, min(1024, lp - 1), ROUTE_ROWS))
    tm_moe = _pick_tile(2 * n_batch * seq, 1024, SWIGLU_ROWS)
    pos, src, tile_expert, tile_valid = _route_plan(route, counts, tm_moe, n_batch, lp)
    x_sorted = h_bf16.at[src].get(mode="promise_in_bounds")
    exp_w = _cast_expert_weights([od_exp_gate[0], od_exp_up[0], od_exp_down[0]])
    y_sorted = _moe_experts(x_sorted, tile_expert, tile_valid, *exp_w, tm_moe, 1792)
    y1 = y_sorted.at[pos[0]].get(mode="promise_in_bounds", unique_indices=True)
    y2 = y_sorted.at[pos[1]].get(mode="promise_in_bounds", unique_indices=True)
    out = _combine_ln(h, y1, y2, route, od_ln2_g[0], od_ln2_b[0], n_batch, lp, _pick_tile(seq, 1024))
    return out.reshape(n_batch, seq, d)
```

```python
import functools
import math

import jax
import jax.numpy as jnp
import numpy as np
from jax import lax
from jax.experimental import pallas as pl
from jax.experimental.pallas import tpu as pltpu

F32 = jnp.float32
BF16 = jnp.bfloat16

D_MODEL = 1024
N_META = 16
ATT_BLOCK = 128
FRONT_PAD = (-N_META) % ATT_BLOCK
CONV_CH = 512
CONV_WIDTH = 31
RWKV_CH = 512
RWKV_HEAD = 64
RWKV_COLS = 3 * RWKV_CH + 64 + 64 + 128
HEAD_DIM = 64
N_Q_HEADS = 16
N_KV_HEADS = 2
GROUP = N_Q_HEADS // N_KV_HEADS
ROPE_THETA = 10000.0
N_EXPERTS = 8
DEPTH = 2
ALPHA = (2 * DEPTH) ** 0.25
LN_EPS = 1e-5
LNX_EPS = 64e-5

CHUNK = 64
RWKV_CHUNKS_PER_STEP = 6
HEADS_PER_GROUP = 2
GROUP_W = HEADS_PER_GROUP * RWKV_HEAD
VMEM_LIMIT = 56 * 1024 * 1024

NN = (((1,), (0,)), ((), ()))
NT = (((1,), (1,)), ((), ()))
TN = (((0,), (0,)), ((), ()))


def _params(*sem):
    return pltpu.CompilerParams(dimension_semantics=sem, vmem_limit_bytes=VMEM_LIMIT)


def _sigmoid(x):
    return 1.0 / (1.0 + jnp.exp(-x))


def _split2(x):
    hi = x.astype(BF16)
    lo = (x - hi.astype(F32)).astype(BF16)
    return hi, lo


def _dot(a, b, dims=NN, passes=1):
    if passes == 1:
        return lax.dot_general(a.astype(BF16), b.astype(BF16), dims, preferred_element_type=F32)
    ah, al = _split2(a)
    bh, bl = _split2(b)
    out = lax.dot_general(ah, bh, dims, preferred_element_type=F32)
    out = out + lax.dot_general(ah, bl, dims, preferred_element_type=F32)
    return out + lax.dot_general(al, bh, dims, preferred_element_type=F32)


def _dot_exact_rhs(a, b_bf16, dims=NN, parts=2):
    out = None
    rem = a
    for _ in range(parts):
        piece = rem.astype(BF16)
        term = lax.dot_general(piece, b_bf16, dims, preferred_element_type=F32)
        out = term if out is None else out + term
        rem = rem - piece.astype(F32)
    return out


def _layer_norm(x, g, b, eps):
    mu = jnp.mean(x, axis=-1, keepdims=True)
    xc = x - mu
    var = jnp.mean(xc * xc, axis=-1, keepdims=True)
    return xc * lax.rsqrt(var + eps) * g + b


CONV_ROWS = 512
CONV_SUB = 32
CONV_HALO = 32
SUBLANES = 8


def _conv_kernel(h_ref, w_ref, cw_ref, cb_ref, g_ref, b_ref, o_ref, buf_ref, sh_ref):
    @pl.when(pl.program_id(0) == 0)
    def _():
        buf_ref[0:CONV_HALO, :] = jnp.zeros((CONV_HALO, CONV_CH), F32)

    n_rows = h_ref.shape[0]
    u = jnp.dot(h_ref[...].astype(BF16), w_ref[...], preferred_element_type=F32)
    buf_ref[CONV_HALO:CONV_HALO + n_rows, :] = u[:, :CONV_CH] * _sigmoid(u[:, CONV_CH:])
    n_sh = CONV_HALO + n_rows - SUBLANES
    for r in range(1, SUBLANES):
        sh_ref[r - 1] = buf_ref[r:r + n_sh, :]
    cw = cw_ref[...]
    base = CONV_HALO - (CONV_WIDTH - 1)
    for s in range(n_rows // CONV_SUB):
        r0 = s * CONV_SUB
        acc = jnp.broadcast_to(cb_ref[...], (CONV_SUB, CONV_CH))
        for j in range(CONV_WIDTH):
            shift = (base + j) % SUBLANES
            start = base + j - shift + r0
            if shift == 0:
                win = buf_ref[start:start + CONV_SUB, :]
            else:
                win = sh_ref[shift - 1, start:start + CONV_SUB, :]
            acc = acc + cw[j:j + 1, :] * win
        y = _layer_norm(acc, g_ref[...], b_ref[...], LN_EPS)
        o_ref[r0:r0 + CONV_SUB, :] = (y * _sigmoid(y)).astype(o_ref.dtype)
    buf_ref[0:CONV_HALO, :] = buf_ref[n_rows:n_rows + CONV_HALO, :]


def _conv_module(h, w_in_conv, conv_w, conv_b, norm_g, norm_b):
    t, d = h.shape
    n_rows = _pick_tile(t, CONV_ROWS, CONV_SUB)
    row = lambda v: v.reshape(1, -1)
    full = lambda shape: pl.BlockSpec(shape, lambda i: (0, 0))
    return pl.pallas_call(
        _conv_kernel,
        grid=(t // n_rows,),
        in_specs=[pl.BlockSpec((n_rows, d), lambda i: (i, 0)), full((d, 2 * CONV_CH)),
                  full((CONV_WIDTH, CONV_CH)), full((1, CONV_CH)), full((1, CONV_CH)),
                  full((1, CONV_CH))],
        out_specs=pl.BlockSpec((n_rows, CONV_CH), lambda i: (i, 0)),
        out_shape=jax.ShapeDtypeStruct((t, CONV_CH), BF16),
        scratch_shapes=[pltpu.VMEM((n_rows + CONV_HALO, CONV_CH), F32),
                        pltpu.VMEM((SUBLANES - 1, n_rows + CONV_HALO - SUBLANES, CONV_CH), F32)],
        compiler_params=_params("arbitrary"),
        name="conv_module",
    )(h, w_in_conv, conv_w, row(conv_b), row(norm_g), row(norm_b))


def _rwkv_masks(step_rows):
    n = GROUP_W
    r = np.arange(n)[:, None]
    c = np.arange(n)[None, :]
    same = lambda w: (r // w) == (c // w)
    strict = (r > c) & same(CHUNK)
    incl = (r >= c) & same(CHUNK)
    m16 = (r > c) & same(16)
    m32 = (r > c) & same(32) & ~same(16)
    m64 = (r > c) & same(64) & ~same(32)
    stack = (c // RWKV_HEAD) == (r // CHUNK)
    masks = np.stack([strict, incl, m16, m32, m64, stack]).astype(np.float32)
    tr = np.arange(step_rows)
    tri = ((tr[:, None] >= tr[None, :]) & ((tr[:, None] // CHUNK) == (tr[None, :] // CHUNK)))
    tri = tri.astype(np.float32)
    hr = np.arange(RWKV_CH)
    head_ones = ((hr[:, None] // RWKV_HEAD) == (hr[None, :] // RWKV_HEAD)).astype(np.float32)
    return masks, tri, head_ones


def _each(f, *cols):
    return [f(*xs) for xs in zip(*cols)]


def _unit_lower_inverse_minus_identity(labs, m16, m32, m64, passes):
    mm = functools.partial(_dot, passes=passes)
    d = [lab * m16 for lab in labs]
    d2 = _each(mm, d, d)
    d4 = _each(mm, d2, d2)
    dd2 = _each(mm, d, d2)
    d8 = _each(mm, d4, d4)
    n = _each(lambda x, x2, xx2: x2 - x - xx2, d, d2, dd2)
    n = _each(lambda x, y, xy: x + y + xy, n, d4, _each(mm, n, d4))
    n = _each(lambda x, y, xy: x + y + xy, n, d8, _each(mm, n, d8))
    for mask in (m32, m64):
        off = [lab * mask for lab in labs]
        w = _each(lambda o, no: o + no, off, _each(mm, n, off))
        w = _each(lambda x, xn: x + xn, w, _each(mm, w, n))
        n = _each(lambda x, y: x - y, n, w)
    return n


def _rwkv_kernel(h_ref, win_ref, mu_ref, w0_ref, w2_ref, a0_ref, a2_ref, g2_ref, kk_ref, ka_ref,
                 rk_ref, lg_ref, lb_ref, masks_ref, tri_ref, ones_ref, o_ref, s_ref, carry_ref):
    c = CHUNK
    rows = h_ref.shape[0]
    n_groups = RWKV_CH // GROUP_W
    w = GROUP_W

    @pl.when(pl.program_id(1) == 0)
    def _():
        s_ref[...] = jnp.zeros_like(s_ref)
        carry_ref[...] = jnp.zeros_like(carry_ref)

    z = jnp.dot(h_ref[...].astype(BF16), win_ref[...], preferred_element_type=F32)
    prev = pltpu.roll(z, 1, 0)
    row_id = lax.broadcasted_iota(jnp.int32, z.shape, 0)
    prev = jnp.where(row_id == 0, jnp.broadcast_to(carry_ref[7:8, :], z.shape), prev)
    carry_ref[...] = z[rows - 8:rows, :]
    z = z + (prev - z) * mu_ref[...]

    r = z[:, 0:RWKV_CH]
    k = z[:, RWKV_CH:2 * RWKV_CH]
    v = z[:, 2 * RWKV_CH:3 * RWKV_CH]
    wa_lo = z[:, 3 * RWKV_CH:3 * RWKV_CH + 128]
    g_lo = z[:, 3 * RWKV_CH + 128:3 * RWKV_CH + 256]

    wpre = w0_ref[...] + _dot(jnp.tanh(wa_lo), w2_ref[...], passes=3)
    ew = _sigmoid(wpre) * math.exp(-0.5)
    a = _sigmoid(a0_ref[...] + _dot(wa_lo, a2_ref[...]))
    gate = _dot(_sigmoid(g_lo), g2_ref[...])

    ones = ones_ref[...]
    kk = k * kk_ref[...]
    ss = _dot_exact_rhs(kk * kk, ones, parts=1)
    kk = kk * lax.rsqrt(jnp.maximum(ss, 1e-24))
    k2 = k * (1.0 + (a - 1.0) * ka_ref[...])
    kka = kk * a

    e0 = ew.astype(BF16)
    e1f = ew - e0.astype(F32)
    e1 = e1f.astype(BF16)
    e2 = (e1f - e1.astype(F32)).astype(BF16)
    tri = tri_ref[...]
    cum = (jnp.dot(tri, e0, preferred_element_type=F32) + jnp.dot(tri, e1, preferred_element_type=F32)
           + jnp.dot(tri, e2, preferred_element_type=F32))
    gam = jnp.exp(-cum)
    igam = jnp.exp(cum)
    r_t = r * gam
    ka_t = kk * jnp.exp(ew - cum)
    b_t = kka * igam
    k_t = k2 * igam

    strict, incl, m16, m32, m64, stack = (masks_ref[i] for i in range(6))
    p1 = 1
    mm = functools.partial(_dot, passes=p1)

    units = [(q, g) for q in range(rows // c) for g in range(n_groups)]

    def stacked(x):
        return [jnp.concatenate([x[q * c:(q + 1) * c, g * w:(g + 1) * w]] * HEADS_PER_GROUP, axis=0)
                * stack for q, g in units]

    xa, xr, xv, yb, yk = stacked(ka_t), stacked(r_t), stacked(v), stacked(b_t), stacked(k_t)
    xar = _each(lambda top, bot: jnp.concatenate([top, bot], axis=0), xa, xr)
    ybk = _each(lambda top, bot: jnp.concatenate([top, bot], axis=0), yb, yk)
    pbk = _each(lambda x, y: mm(x, y, NT), xar, ybk)
    lab = [p[:w, :w] * strict for p in pbk]
    arb = [p[w:, :w] * incl for p in pbk]
    lak = [p[:w, w:] * strict for p in pbk]
    ark = [p[w:, w:] * incl for p in pbk]
    n_inv = _unit_lower_inverse_minus_identity(lab, m16, m32, m64, p1)
    lv = _each(mm, lak, xv)
    rhs = _each(lambda left, right: jnp.concatenate([left, right], axis=1), xa, lv)
    gz = _each(lambda x, nx: x + nx, rhs, _each(mm, n_inv, rhs))
    corr = _each(mm, arb, gz)
    qh = _each(lambda x, cr: x - cr[:, :w], xr, corr)
    yloc = _each(lambda av, cr: av - cr[:, w:], _each(mm, ark, xv), corr)
    gzb = _each(lambda x, y: mm(x, y, TN), gz, yb)
    vk = _each(lambda x, y: mm(x, y, TN), xv, yk)

    state = [s_ref[g] for g in range(n_groups)]
    y_rows = []
    for q in range(rows // c):
        gam_end = gam[(q + 1) * c - 1:(q + 1) * c, :]
        ys = []
        for g in range(n_groups):
            i = q * n_groups + g
            s0 = state[g]
            y_st = mm(qh[i], s0, NT) + yloc[i]
            y_heads = y_st[0:c]
            for h in range(1, HEADS_PER_GROUP):
                y_heads = y_heads + y_st[h * c:(h + 1) * c]
            ys.append(y_heads)
            s_new = s0 - mm(s0, gzb[i][:w]) + vk[i] - gzb[i][w:]
            state[g] = s_new * gam_end[:, g * w:(g + 1) * w]
        y_rows.append(jnp.concatenate(ys, axis=1))
    for g in range(n_groups):
        s_ref[g] = state[g]
    y = jnp.concatenate(y_rows, axis=0)

    inv_n = 1.0 / RWKV_HEAD
    mean = _dot_exact_rhs(y, ones) * inv_n
    yc = y - mean
    var = _dot_exact_rhs(yc * yc, ones, parts=1) * inv_n
    yn = yc * lax.rsqrt(var + LNX_EPS) * lg_ref[...] + lb_ref[...]
    bonus = _dot_exact_rhs(r * k2 * rk_ref[...], ones, parts=1) * v
    o_ref[...] = ((yn + bonus) * gate).astype(o_ref.dtype)


def _rwkv_time_mix(h, w_in_rwkv, n_batch, shift_mu, w0, w2, a0, a2, g2, k_k, k_a, r_k, lnx_g, lnx_b):
    t, d = h.shape
    step_rows = CHUNK * RWKV_CHUNKS_PER_STEP
    n_steps = t // n_batch // step_rows
    masks, tri, head_ones = _rwkv_masks(step_rows)
    row = lambda v: v.reshape(1, -1).astype(F32)
    zeros64 = jnp.zeros((64, RWKV_CH), F32)
    w2p = jnp.concatenate([w2, zeros64], axis=0)
    a2p = jnp.concatenate([zeros64, a2], axis=0)
    full2 = lambda shape: pl.BlockSpec(shape, lambda b, c: (0, 0))
    vec = full2((1, RWKV_CH))
    return pl.pallas_call(
        _rwkv_kernel,
        grid=(n_batch, n_steps),
        in_specs=[pl.BlockSpec((step_rows, d), lambda b, c: (b * n_steps + c, 0)),
                  full2((d, RWKV_COLS)), full2((1, RWKV_COLS)), vec, full2((128, RWKV_CH)), vec, full2((128, RWKV_CH)),
                  full2((128, RWKV_CH)), vec, vec, vec, vec, vec,
                  pl.BlockSpec(masks.shape, lambda b, c: (0, 0, 0)),
                  full2((step_rows, step_rows)), full2((RWKV_CH, RWKV_CH))],
        out_specs=pl.BlockSpec((step_rows, RWKV_CH), lambda b, c: (b * n_steps + c, 0)),
        out_shape=jax.ShapeDtypeStruct((t, RWKV_CH), BF16),
        scratch_shapes=[pltpu.VMEM((RWKV_CH // GROUP_W, GROUP_W, GROUP_W), F32),
                        pltpu.VMEM((8, RWKV_COLS), F32)],
        compiler_params=_params("arbitrary", "arbitrary"),
        name="rwkv7_time_mix",
    )(h, w_in_rwkv, row(shift_mu), row(w0), w2p, row(a0), a2p, g2, row(k_k), row(k_a), row(r_k),
      row(lnx_g), row(lnx_b), jnp.asarray(masks), jnp.asarray(tri, BF16),
      jnp.asarray(head_ones, BF16))


SWIGLU_ROWS = 256


def _swiglu_rows(x_ref, wg_ref, wu_ref, wd_ref, rows):
    xb = x_ref[rows, :].astype(BF16)
    gate = jnp.dot(xb, wg_ref[...], preferred_element_type=F32)
    up = jnp.dot(xb, wu_ref[...], preferred_element_type=F32)
    act = (gate * _sigmoid(gate) * up).astype(BF16)
    return jnp.dot(act, wd_ref[...], preferred_element_type=F32)


def _row_blocks(n_rows):
    assert n_rows % SWIGLU_ROWS == 0
    return [slice(r0, r0 + SWIGLU_ROWS) for r0 in range(0, n_rows, SWIGLU_ROWS)]


def _mix_ffn_kernel(h_ref, a_ref, b_ref, wa_ref, wb_ref, g1_ref, b1_ref, wg_ref, wu_ref, wd_ref,
                    g2_ref, b2_ref, o_ref, h1_ref):
    for rows in _row_blocks(h_ref.shape[0]):
        mix = (jnp.dot(a_ref[rows, :], wa_ref[...], preferred_element_type=F32)
               + jnp.dot(b_ref[rows, :], wb_ref[...], preferred_element_type=F32))
        h1 = _layer_norm(ALPHA * h_ref[rows, :] + mix, g1_ref[...], b1_ref[...], LN_EPS)
        h1_ref[rows, :] = h1
        acc = ALPHA * h1 + _swiglu_rows(h1_ref, wg_ref, wu_ref, wd_ref, rows)
        o_ref[rows, :] = _layer_norm(acc, g2_ref[...], b2_ref[...], LN_EPS)


def _mix_ffn(h, a_out, b_out, w_a, w_b, g1, b1, wg, wu, wd, g2, b2, tm):
    t, d = h.shape
    dff = wg.shape[1]
    row = lambda v: v.reshape(1, -1)
    full = lambda shape: pl.BlockSpec(shape, lambda i: (0, 0))
    tile = lambda width: pl.BlockSpec((tm, width), lambda i: (i, 0))
    once = lambda shape: pl.BlockSpec(shape, lambda i: (0, 0), pipeline_mode=pl.Buffered(1))
    return pl.pallas_call(
        _mix_ffn_kernel,
        grid=(t // tm,),
        in_specs=[tile(d), tile(a_out.shape[1]), tile(b_out.shape[1]), full(w_a.shape), full(w_b.shape),
                  full((1, d)), full((1, d)), once((d, dff)), once((d, dff)), once((dff, d)),
                  full((1, d)), full((1, d))],
        out_specs=tile(d),
        out_shape=jax.ShapeDtypeStruct((t, d), F32),
        scratch_shapes=[pltpu.VMEM((tm, d), F32)],
        compiler_params=_params("parallel"),
        name="mix_ffn_ln",
    )(h, a_out, b_out, w_a, w_b, row(g1), row(b1), wg, wu, wd, row(g2), row(b2))


def _cast_kernel(*refs):
    n = len(refs) // 2
    for src, dst in zip(refs[:n], refs[n:]):
        dst[...] = src[...].astype(dst.dtype)


CAST_STEPS = 32


def _cast_expert_weights(ws):
    flat = [w.reshape(-1, w.shape[-1]) for w in ws]
    assert all(f.shape[0] % (8 * CAST_STEPS) == 0 for f in flat)
    specs = [pl.BlockSpec((f.shape[0] // CAST_STEPS, f.shape[1]), lambda i: (i, 0)) for f in flat]
    outs = pl.pallas_call(
        _cast_kernel,
        grid=(CAST_STEPS,),
        in_specs=specs,
        out_specs=specs,
        out_shape=[jax.ShapeDtypeStruct(f.shape, BF16) for f in flat],
        compiler_params=_params("parallel"),
        name="cast_bf16",
    )(*flat)
    return [o.reshape(w.shape) for o, w in zip(outs, ws)]


Q_COLS = N_Q_HEADS * HEAD_DIM
K_COLS = 2 * N_KV_HEADS * 128
V_COLS = N_KV_HEADS * 128
KV_COLS = K_COLS + V_COLS


def _qkv_rope_kernel(h_ref, w_ref, bias_ref, cos_ref, sin_ref, q_ref, kv_ref):
    acc = jnp.dot(h_ref[...].astype(BF16), w_ref[...], preferred_element_type=F32) + bias_ref[...]
    cos = cos_ref[...]
    sin = sin_ref[...]
    lane = lax.broadcasted_iota(jnp.int32, cos.shape, 1)
    first_half = (lane % HEAD_DIM) < (HEAD_DIM // 2)
    for c0 in range(0, Q_COLS + K_COLS, 128):
        x = acc[:, c0:c0 + 128]
        swapped = jnp.where(first_half, pltpu.roll(x, 128 - HEAD_DIM // 2, 1),
                            pltpu.roll(x, HEAD_DIM // 2, 1))
        y = x * cos + swapped * sin
        if c0 < Q_COLS:
            q_ref[:, c0:c0 + 128] = (y * (HEAD_DIM ** -0.5)).astype(BF16)
        else:
            kv_ref[:, c0 - Q_COLS:c0 - Q_COLS + 128] = y.astype(BF16)
    kv_ref[:, K_COLS:] = acc[:, Q_COLS + K_COLS:].astype(BF16)


def _rope_tables(lp):
    half = HEAD_DIM // 2
    inv = ROPE_THETA ** (-jnp.arange(half, dtype=F32) / half)
    pos = (jnp.arange(lp) - FRONT_PAD).astype(F32)
    ang = pos[:, None] * inv[None, :]
    cos = jnp.cos(ang)
    sin = jnp.sin(ang)
    cos = jnp.concatenate([cos, cos] * (128 // HEAD_DIM), axis=1)
    sin = jnp.concatenate([-sin, sin] * (128 // HEAD_DIM), axis=1)
    return cos, sin


def _qkv_rope(h, w, bias, lp, tm):
    t, d = h.shape
    n = w.shape[1]
    cos, sin = _rope_tables(lp)
    per_seq = lp // tm
    return pl.pallas_call(
        _qkv_rope_kernel,
        grid=(t // tm,),
        in_specs=[pl.BlockSpec((tm, d), lambda i: (i, 0)),
                  pl.BlockSpec((d, n), lambda i: (0, 0)),
                  pl.BlockSpec((1, n), lambda i: (0, 0)),
                  pl.BlockSpec((tm, 128), lambda i: (i % per_seq, 0)),
                  pl.BlockSpec((tm, 128), lambda i: (i % per_seq, 0))],
        out_specs=[pl.BlockSpec((tm, Q_COLS), lambda i: (i, 0)),
                   pl.BlockSpec((tm, KV_COLS), lambda i: (i, 0))],
        out_shape=[jax.ShapeDtypeStruct((t, Q_COLS), BF16), jax.ShapeDtypeStruct((t, KV_COLS), BF16)],
        compiler_params=_params("parallel"),
        name="qkv_rope",
    )(h, w, bias.reshape(1, -1), cos, sin)


def _qkv_weights(w_qkv, b_qkv):
    kv_w = N_KV_HEADS * HEAD_DIM
    wq, wk, wv = jnp.split(w_qkv, [Q_COLS, Q_COLS + kv_w], axis=1)
    bq, bk, bv = jnp.split(b_qkv[None], [Q_COLS, Q_COLS + kv_w], axis=1)

    def expand(m):
        zero = jnp.zeros((m.shape[0], HEAD_DIM), m.dtype)
        k_cols, v_cols = [], []
        for g in range(N_KV_HEADS):
            col = m[:, g * HEAD_DIM:(g + 1) * HEAD_DIM]
            k_cols += [col, zero, zero, col]
            v_cols += [col, col]
        return k_cols, v_cols

    w_cols = [wq] + expand(wk)[0] + expand(wv)[1]
    b_cols = [bq] + expand(bk)[0] + expand(bv)[1]
    return jnp.concatenate(w_cols, axis=1), jnp.concatenate(b_cols, axis=1)[0]


NEG = -1e30


def _attn_kernel(sink_ref, q_ref, kv_prev_ref, kv_cur_ref, o_ref):
    n = pl.program_id(1)
    blk = ATT_BLOCK
    kv = jnp.concatenate([kv_prev_ref[...], kv_cur_ref[...]], axis=0)
    qi = lax.broadcasted_iota(jnp.int32, (blk, 2 * blk), 0)
    kj = lax.broadcasted_iota(jnp.int32, (blk, 2 * blk), 1)
    first_key = FRONT_PAD - (n - 1) * blk
    valid = (kj > qi) & (kj <= qi + blk) & (kj >= first_key)
    low = lax.broadcasted_iota(jnp.int32, (blk, 128), 1) < HEAD_DIM
    for pair in range(N_Q_HEADS // 2):
        grp = (2 * pair) // GROUP
        vd = kv[:, K_COLS + grp * 128:K_COLS + (grp + 1) * 128]
        qp = q_ref[:, pair * 128:(pair + 1) * 128]
        outs = []
        for half in range(2):
            head = 2 * pair + half
            kh = kv[:, (2 * grp + half) * 128:(2 * grp + half + 1) * 128]
            s = lax.dot_general(qp, kh, NT, preferred_element_type=F32)
            s = jnp.where(valid, s, NEG)
            sink = sink_ref[head]
            m = jnp.maximum(jnp.max(s, axis=-1, keepdims=True), sink)
            p = jnp.exp(s - m)
            denom = jnp.sum(p, axis=-1, keepdims=True) + jnp.exp(sink - m)
            o = jnp.dot(p.astype(BF16), vd, preferred_element_type=F32)
            outs.append(o * (1.0 / denom))
        o_ref[:, pair * 128:(pair + 1) * 128] = jnp.where(low, outs[0], outs[1]).astype(o_ref.dtype)


def _attention(q, kv, sinks, n_batch, lp):
    t = q.shape[0]
    nb = lp // ATT_BLOCK
    return pl.pallas_call(
        _attn_kernel,
        grid=(n_batch, nb),
        in_specs=[pl.BlockSpec(memory_space=pltpu.SMEM),
                  pl.BlockSpec((ATT_BLOCK, Q_COLS), lambda b, n: (b * nb + n, 0)),
                  pl.BlockSpec((ATT_BLOCK, KV_COLS), lambda b, n: (b * nb + jnp.maximum(n - 1, 0), 0)),
                  pl.BlockSpec((ATT_BLOCK, KV_COLS), lambda b, n: (b * nb + n, 0))],
        out_specs=pl.BlockSpec((ATT_BLOCK, Q_COLS), lambda b, n: (b * nb + n, 0)),
        out_shape=jax.ShapeDtypeStruct((t, Q_COLS), BF16),
        compiler_params=_params("parallel", "parallel"),
        name="swa_sink_attention",
    )(sinks.astype(F32), q, kv, kv)


ROW_E0, ROW_E1, ROW_W0, ROW_W1, ROW_R0, ROW_R1 = range(6)
ROUTE_FIELDS = 16
ROUTE_ROWS = 256


def _oproj_ln_route_kernel(lp, h_ref, x_ref, w_ref, bias_ref, g_ref, b_ref, rwt_ref, tri_ref,
                           o_ref, ob_ref, route_ref, count_ref, run_ref):
    @pl.when(pl.program_id(0) == 0)
    def _():
        run_ref[...] = jnp.zeros_like(run_ref)

    tm = h_ref.shape[0]
    for r0 in range(0, tm, ROUTE_ROWS):
        rows = slice(r0, r0 + ROUTE_ROWS)
        acc = ALPHA * h_ref[rows, :] + bias_ref[...] + jnp.dot(
            x_ref[rows, :], w_ref[...], preferred_element_type=F32)
        hn = _layer_norm(acc, g_ref[...], b_ref[...], LN_EPS)
        o_ref[rows, :] = hn
        ob_ref[rows, :] = hn.astype(BF16)

    sub = lax.broadcasted_iota(jnp.int32, (ROUTE_FIELDS, tm), 0)
    lane = lax.broadcasted_iota(jnp.int32, (ROUTE_FIELDS, tm), 1)
    logits = _dot(rwt_ref[...], o_ref[...], NT, passes=3)
    logits = jnp.where(sub < N_EXPERTS, logits, -jnp.inf)
    v1 = jnp.max(logits, axis=0, keepdims=True)
    i1 = jnp.min(jnp.where(logits == v1, sub, ROUTE_FIELDS), axis=0, keepdims=True)
    rest = jnp.where(sub == i1, -jnp.inf, logits)
    v2 = jnp.max(rest, axis=0, keepdims=True)
    i2 = jnp.min(jnp.where(rest == v2, sub, ROUTE_FIELDS), axis=0, keepdims=True)
    w1 = 1.0 / (1.0 + jnp.exp(v2 - v1))
    w2 = 1.0 - w1

    first = pl.program_id(0) * tm
    lead = ((first + tm - 1) // lp) * lp
    row = first + lane
    routed = (row < lead) | (row >= lead + FRONT_PAD + N_META)

    run = run_ref[:, 0:1]
    oh1 = jnp.where(routed & (sub == i1), 1.0, 0.0)
    oh2 = jnp.where(routed & (sub == i2), 1.0, 0.0)
    both = oh1 + oh2
    before = jnp.dot(both.astype(BF16), tri_ref[...], preferred_element_type=F32) + run
    r1 = jnp.sum(oh1 * before, axis=0, keepdims=True)
    r2 = jnp.sum(oh2 * before, axis=0, keepdims=True)
    run = run + jnp.sum(both, axis=1, keepdims=True)

    out = jnp.where(sub == ROW_E0, i1.astype(F32), 0.0)
    out = jnp.where(sub == ROW_E1, i2.astype(F32), out)
    out = jnp.where(sub == ROW_W0, w1, out)
    out = jnp.where(sub == ROW_W1, w2, out)
    out = jnp.where(sub == ROW_R0, r1, out)
    out = jnp.where(sub == ROW_R1, r2, out)
    route_ref[...] = out
    run_ref[...] = jnp.broadcast_to(run, run_ref.shape)
    count_ref[...] = jnp.broadcast_to(run, count_ref.shape)


def _oproj_ln_route(h, x, w, bias, g, b, router_w, lp, tm):
    t, d = h.shape
    assert tm % ROUTE_ROWS == 0 and lp > tm
    row = lambda v: v.reshape(1, -1)
    rwt = jnp.zeros((ROUTE_FIELDS, d), F32).at[:N_EXPERTS].set(router_w.T)
    tri = jnp.asarray(np.triu(np.ones((tm, tm), np.float32), 1), BF16)
    tile = lambda width: pl.BlockSpec((tm, width), lambda i: (i, 0))
    full = lambda shape: pl.BlockSpec(shape, lambda i: (0, 0))
    return pl.pallas_call(
        functools.partial(_oproj_ln_route_kernel, lp),
        grid=(t // tm,),
        in_specs=[tile(d), tile(x.shape[1]), full(w.shape), full((1, d)), full((1, d)), full((1, d)),
                  full((ROUTE_FIELDS, d)), full((tm, tm))],
        out_specs=[tile(d), tile(d), pl.BlockSpec((ROUTE_FIELDS, tm), lambda i: (0, i)),
                   full((ROUTE_FIELDS, 128))],
        out_shape=[jax.ShapeDtypeStruct((t, d), F32), jax.ShapeDtypeStruct((t, d), BF16),
                   jax.ShapeDtypeStruct((ROUTE_FIELDS, t), F32),
                   jax.ShapeDtypeStruct((ROUTE_FIELDS, 128), F32)],
        scratch_shapes=[pltpu.VMEM((ROUTE_FIELDS, 128), F32)],
        compiler_params=_params("arbitrary"),
        name="oproj_ln_route",
    )(h, x, w, row(bias), row(g), row(b), rwt, tri)


def _moe_kernel(n_ff, te_ref, tv_ref, x_ref, wg_ref, wu_ref, wd_ref, o_ref, acc_ref):
    i = pl.program_id(0)
    j = pl.program_id(1)
    last = n_ff - 1

    def step(first, final):
        for rows in _row_blocks(x_ref.shape[0]):
            part = _swiglu_rows(x_ref, wg_ref.at[0], wu_ref.at[0], wd_ref.at[0], rows)
            if not first:
                part = acc_ref[rows, :] + part
            if final:
                o_ref[rows, :] = part.astype(o_ref.dtype)
            else:
                acc_ref[rows, :] = part

    valid = tv_ref[i] > 0
    pl.when(valid & (j == 0))(functools.partial(step, True, False))
    if n_ff > 2:
        pl.when(valid & (j > 0) & (j < last))(functools.partial(step, False, False))
    pl.when(valid & (j == last))(functools.partial(step, False, True))

    @pl.when(jnp.logical_not(valid) & (j == last))
    def _():
        o_ref[...] = jnp.zeros_like(o_ref)


def _moe_experts(x_sorted, tile_expert, tile_valid, wg, wu, wd, tm, tf):
    n_rows, d = x_sorted.shape
    dff = wg.shape[2]
    ff_tile = lambda i, j, tv: j * tv[i] + (dff // tf - 1) * (1 - tv[i])
    grid_spec = pltpu.PrefetchScalarGridSpec(
        num_scalar_prefetch=2,
        grid=(n_rows // tm, dff // tf),
        in_specs=[pl.BlockSpec((tm, d), lambda i, j, te, tv: (i, 0)),
                  pl.BlockSpec((1, d, tf), lambda i, j, te, tv: (te[i], 0, ff_tile(i, j, tv))),
                  pl.BlockSpec((1, d, tf), lambda i, j, te, tv: (te[i], 0, ff_tile(i, j, tv))),
                  pl.BlockSpec((1, tf, d), lambda i, j, te, tv: (te[i], ff_tile(i, j, tv), 0))],
        out_specs=pl.BlockSpec((tm, d), lambda i, j, te, tv: (i, 0)),
        scratch_shapes=[pltpu.VMEM((tm, d), F32)],
    )
    assert dff // tf >= 2
    return pl.pallas_call(
        functools.partial(_moe_kernel, dff // tf),
        grid_spec=grid_spec,
        out_shape=jax.ShapeDtypeStruct((n_rows, d), BF16),
        compiler_params=_params("parallel", "arbitrary"),
        name="moe_expert_swiglu",
    )(tile_expert, tile_valid, x_sorted, wg, wu, wd)


def _route_plan(route, counts, tm, n_batch, lp):
    lead = FRONT_PAD + N_META
    seq_route = route.reshape(ROUTE_FIELDS, n_batch, lp)[:ROW_R1 + 1, :, lead:]
    seq_route = seq_route.reshape(ROW_R1 + 1, -1).astype(jnp.int32)
    top_idx = seq_route[ROW_E0:ROW_E1 + 1]
    rank = seq_route[ROW_R0:ROW_R1 + 1]
    n_assign = top_idx.size
    n_rows = n_assign + N_EXPERTS * tm
    counts = counts[:N_EXPERTS, 0].astype(jnp.int32)
    padded = ((counts + tm - 1) // tm) * tm
    ends = jnp.cumsum(padded)
    starts = ends - padded
    start_of = jnp.zeros_like(top_idx)
    for e in range(N_EXPERTS):
        start_of = jnp.where(top_idx == e, starts[e], start_of)
    pos = start_of + rank
    token = (jnp.arange(n_batch, dtype=jnp.int32)[:, None] * lp + lead
             + jnp.arange(lp - lead, dtype=jnp.int32)[None, :]).reshape(-1)
    src = jnp.arange(n_rows, dtype=jnp.int32) % (n_batch * lp)
    src = src.at[pos.reshape(-1)].set(jnp.tile(token, 2), unique_indices=True,
                                      mode="promise_in_bounds")
    tile_start = jnp.arange(n_rows // tm, dtype=jnp.int32) * tm
    tile_expert = jnp.sum((tile_start[:, None] >= ends[None, :]).astype(jnp.int32), axis=1)
    tile_expert = jnp.minimum(tile_expert, N_EXPERTS - 1)
    tile_valid = (tile_start < ends[-1]).astype(jnp.int32)
    return pos, src, tile_expert, tile_valid


def _combine_ln_kernel(h_ref, y1_ref, y2_ref, r_ref, sel_ref, g_ref, b_ref, o_ref):
    wts = None
    rem = r_ref[...]
    for _ in range(3):
        piece = rem.astype(BF16)
        term = lax.dot_general(piece, sel_ref[...], TN, preferred_element_type=F32)
        wts = term if wts is None else wts + term
        rem = rem - piece.astype(F32)
    reps = h_ref.shape[1] // 128
    w0 = jnp.concatenate([wts[:, :128]] * reps, axis=1)
    w1 = jnp.concatenate([wts[:, 128:]] * reps, axis=1)
    acc = ALPHA * h_ref[...] + w0 * y1_ref[...].astype(F32) + w1 * y2_ref[...].astype(F32)
    o_ref[...] = _layer_norm(acc, g_ref[...], b_ref[...], LN_EPS)


def _combine_ln(h, y1, y2, route, g, b, n_batch, lp, tm):
    d = h.shape[1]
    seq = lp - FRONT_PAD - N_META
    per_seq = seq // tm
    row = lambda v: v.reshape(1, -1)
    first_row = lambda b, j: pl.multiple_of(b * lp + FRONT_PAD + N_META + j * tm, ATT_BLOCK)
    h_tile = pl.BlockSpec((pl.Element(tm), pl.Element(d)), lambda b, j: (first_row(b, j), 0))
    route_tile = pl.BlockSpec((pl.Element(ROUTE_FIELDS), pl.Element(tm)),
                              lambda b, j: (0, first_row(b, j)))
    tile = pl.BlockSpec((tm, d), lambda b, j: (b * per_seq + j, 0))
    vec = pl.BlockSpec((1, d), lambda b, j: (0, 0))
    sel = np.zeros((ROUTE_FIELDS, 256), np.float32)
    sel[ROW_W0, :128] = 1.0
    sel[ROW_W1, 128:] = 1.0
    return pl.pallas_call(
        _combine_ln_kernel,
        grid=(n_batch, per_seq),
        in_specs=[h_tile, tile, tile, route_tile, pl.BlockSpec(sel.shape, lambda b, j: (0, 0)),
                  vec, vec],
        out_specs=tile,
        out_shape=jax.ShapeDtypeStruct((n_batch * seq, d), F32),
        compiler_params=_params("parallel", "parallel"),
        name="moe_combine_ln",
    )(h, y1, y2, route, jnp.asarray(sel, BF16), row(g), row(b))


def _pick_tile(n, target, mult=16):
    best = mult
    for cand in range(mult, min(n, target) + 1, mult):
        if n % cand == 0:
            best = cand
    return best


def kernel(x, meta_tokens, ev_w_in, ev_conv_w, ev_conv_b, ev_convnorm_g, ev_convnorm_b, ev_shift_mu, ev_w0, ev_w2, ev_a0, ev_a2, ev_g2, ev_k_k, ev_k_a, ev_r_k, ev_lnx_g, ev_lnx_b, ev_w_out, ev_ln1_g, ev_ln1_b, ev_ffn_gate, ev_ffn_up, ev_ffn_down, ev_ln2_g, ev_ln2_b, od_w_qkv, od_b_qkv, od_sinks, od_w_o, od_b_o, od_ln1_g, od_ln1_b, od_router, od_exp_gate, od_exp_up, od_exp_down, od_ln2_g, od_ln2_b):
    n_batch, seq, d = x.shape
    lp = FRONT_PAD + N_META + seq
    assert d == D_MODEL and lp % ATT_BLOCK == 0 and lp % (CHUNK * RWKV_CHUNKS_PER_STEP) == 0
    t = n_batch * lp
    tm = _pick_tile(t, 1024)
    tm_seq = _pick_tile(lp, 1056)

    meta = jnp.broadcast_to(meta_tokens[None].astype(x.dtype), (n_batch, N_META, d))
    h = jnp.concatenate([jnp.zeros((n_batch, FRONT_PAD, d), x.dtype), meta, x], axis=1)
    h = h.reshape(t, d)

    w_in = ev_w_in[0].astype(BF16)
    a_out = _conv_module(h, w_in[:, :2 * CONV_CH], ev_conv_w[0], ev_conv_b[0], ev_convnorm_g[0],
                         ev_convnorm_b[0])
    b_out = _rwkv_time_mix(h, w_in[:, 2 * CONV_CH:], n_batch, ev_shift_mu[0], ev_w0[0], ev_w2[0],
                           ev_a0[0], ev_a2[0], ev_g2[0], ev_k_k[0], ev_k_a[0], ev_r_k[0],
                           ev_lnx_g[0], ev_lnx_b[0])
    w_out = ev_w_out[0].astype(BF16)
    h = _mix_ffn(h, a_out, b_out, w_out[:CONV_CH], w_out[CONV_CH:], ev_ln1_g[0], ev_ln1_b[0],
                 ev_ffn_gate[0].astype(BF16), ev_ffn_up[0].astype(BF16), ev_ffn_down[0].astype(BF16),
                 ev_ln2_g[0], ev_ln2_b[0], _pick_tile(t, 1024, SWIGLU_ROWS))

    w_qkv, b_qkv = _qkv_weights(od_w_qkv[0], od_b_qkv[0])
    q, kv = _qkv_rope(h, w_qkv.astype(BF16), b_qkv, lp, tm_seq)
    att = _attention(q, kv, od_sinks[0], n_batch, lp)
    h, h_bf16, route, counts = _oproj_ln_route(h, att, od_w_o[0].astype(BF16), od_b_o[0], od_ln1_g[0],
                                               od_ln1_b[0], od_router[0], lp,
                                               _pick_tile(t, min(1024, lp - 1), ROUTE_ROWS))
    tm_moe = _pick_tile(2 * n_batch * seq, 1024, SWIGLU_ROWS)
    pos, src, tile_expert, tile_valid = _route_plan(route, counts, tm_moe, n_batch, lp)
    x_sorted = h_bf16.at[src].get(mode="promise_in_bounds")
    exp_w = _cast_expert_weights([od_exp_gate[0], od_exp_up[0], od_exp_down[0]])
    y_sorted = _moe_experts(x_sorted, tile_expert, tile_valid, *exp_w, tm_moe, 1792)
    y1 = y_sorted.at[pos[0]].get(mode="promise_in_bounds", unique_indices=True)
    y2 = y_sorted.at[pos[1]].get(mode="promise_in_bounds", unique_indices=True)
    out = _combine_ln(h, y1, y2, route, od_ln2_g[0], od_ln2_b[0], n_batch, lp, _pick_tile(seq, 1024))
    return out.reshape(n_batch, seq, d)
```

```python
import functools
import math

import jax
import jax.numpy as jnp
import numpy as np
from jax import lax
from jax.experimental import pallas as pl
from jax.experimental.pallas import tpu as pltpu

F32 = jnp.float32
BF16 = jnp.bfloat16

D_MODEL = 1024
N_META = 16
ATT_BLOCK = 128
FRONT_PAD = (-N_META) % ATT_BLOCK
CONV_CH = 512
CONV_WIDTH = 31
RWKV_CH = 512
RWKV_HEAD = 64
RWKV_COLS = 3 * RWKV_CH + 64 + 64 + 128
HEAD_DIM = 64
N_Q_HEADS = 16
N_KV_HEADS = 2
GROUP = N_Q_HEADS // N_KV_HEADS
ROPE_THETA = 10000.0
N_EXPERTS = 8
DEPTH = 2
ALPHA = (2 * DEPTH) ** 0.25
LN_EPS = 1e-5
LNX_EPS = 64e-5

CHUNK = 64
RWKV_CHUNKS_PER_STEP = 6
HEADS_PER_GROUP = 2
GROUP_W = HEADS_PER_GROUP * RWKV_HEAD
VMEM_LIMIT = 56 * 1024 * 1024

NN = (((1,), (0,)), ((), ()))
NT = (((1,), (1,)), ((), ()))
TN = (((0,), (0,)), ((), ()))


def _params(*sem):
    return pltpu.CompilerParams(dimension_semantics=sem, vmem_limit_bytes=VMEM_LIMIT)


def _sigmoid(x):
    return 1.0 / (1.0 + jnp.exp(-x))


def _split2(x):
    hi = x.astype(BF16)
    lo = (x - hi.astype(F32)).astype(BF16)
    return hi, lo


def _dot(a, b, dims=NN, passes=1):
    if passes == 1:
        return lax.dot_general(a.astype(BF16), b.astype(BF16), dims, preferred_element_type=F32)
    ah, al = _split2(a)
    bh, bl = _split2(b)
    out = lax.dot_general(ah, bh, dims, preferred_element_type=F32)
    out = out + lax.dot_general(ah, bl, dims, preferred_element_type=F32)
    return out + lax.dot_general(al, bh, dims, preferred_element_type=F32)


def _dot_exact_rhs(a, b_bf16, dims=NN, parts=2):
    out = None
    rem = a
    for _ in range(parts):
        piece = rem.astype(BF16)
        term = lax.dot_general(piece, b_bf16, dims, preferred_element_type=F32)
        out = term if out is None else out + term
        rem = rem - piece.astype(F32)
    return out


def _layer_norm(x, g, b, eps):
    mu = jnp.mean(x, axis=-1, keepdims=True)
    xc = x - mu
    var = jnp.mean(xc * xc, axis=-1, keepdims=True)
    return xc * lax.rsqrt(var + eps) * g + b


CONV_ROWS = 512
CONV_SUB = 32
CONV_HALO = 32
SUBLANES = 8


def _conv_kernel(h_ref, w_ref, cw_ref, cb_ref, g_ref, b_ref, o_ref, buf_ref, sh_ref):
    @pl.when(pl.program_id(0) == 0)
    def _():
        buf_ref[0:CONV_HALO, :] = jnp.zeros((CONV_HALO, CONV_CH), F32)

    n_rows = h_ref.shape[0]
    u = jnp.dot(h_ref[...].astype(BF16), w_ref[...], preferred_element_type=F32)
    buf_ref[CONV_HALO:CONV_HALO + n_rows, :] = u[:, :CONV_CH] * _sigmoid(u[:, CONV_CH:])
    n_sh = CONV_HALO + n_rows - SUBLANES
    for r in range(1, SUBLANES):
        sh_ref[r - 1] = buf_ref[r:r + n_sh, :]
    cw = cw_ref[...]
    base = CONV_HALO - (CONV_WIDTH - 1)
    for s in range(n_rows // CONV_SUB):
        r0 = s * CONV_SUB
        acc = jnp.broadcast_to(cb_ref[...], (CONV_SUB, CONV_CH))
        for j in range(CONV_WIDTH):
            shift = (base + j) % SUBLANES
            start = base + j - shift + r0
            if shift == 0:
                win = buf_ref[start:start + CONV_SUB, :]
            else:
                win = sh_ref[shift - 1, start:start + CONV_SUB, :]
            acc = acc + cw[j:j + 1, :] * win
        y = _layer_norm(acc, g_ref[...], b_ref[...], LN_EPS)
        o_ref[r0:r0 + CONV_SUB, :] = (y * _sigmoid(y)).astype(o_ref.dtype)
    buf_ref[0:CONV_HALO, :] = buf_ref[n_rows:n_rows + CONV_HALO, :]


def _conv_module(h, w_in_conv, conv_w, conv_b, norm_g, norm_b):
    t, d = h.shape
    n_rows = _pick_tile(t, CONV_ROWS, CONV_SUB)
    row = lambda v: v.reshape(1, -1)
    full = lambda shape: pl.BlockSpec(shape, lambda i: (0, 0))
    return pl.pallas_call(
        _conv_kernel,
        grid=(t // n_rows,),
        in_specs=[pl.BlockSpec((n_rows, d), lambda i: (i, 0)), full((d, 2 * CONV_CH)),
                  full((CONV_WIDTH, CONV_CH)), full((1, CONV_CH)), full((1, CONV_CH)),
                  full((1, CONV_CH))],
        out_specs=pl.BlockSpec((n_rows, CONV_CH), lambda i: (i, 0)),
        out_shape=jax.ShapeDtypeStruct((t, CONV_CH), BF16),
        scratch_shapes=[pltpu.VMEM((n_rows + CONV_HALO, CONV_CH), F32),
                        pltpu.VMEM((SUBLANES - 1, n_rows + CONV_HALO - SUBLANES, CONV_CH), F32)],
        compiler_params=_params("arbitrary"),
        name="conv_module",
    )(h, w_in_conv, conv_w, row(conv_b), row(norm_g), row(norm_b))


def _rwkv_masks(step_rows):
    n = GROUP_W
    r = np.arange(n)[:, None]
    c = np.arange(n)[None, :]
    same = lambda w: (r // w) == (c // w)
    strict = (r > c) & same(CHUNK)
    incl = (r >= c) & same(CHUNK)
    m16 = (r > c) & same(16)
    m32 = (r > c) & same(32) & ~same(16)
    m64 = (r > c) & same(64) & ~same(32)
    stack = (c // RWKV_HEAD) == (r // CHUNK)
    masks = np.stack([strict, incl, m16, m32, m64, stack]).astype(np.float32)
    tr = np.arange(step_rows)
    tri = ((tr[:, None] >= tr[None, :]) & ((tr[:, None] // CHUNK) == (tr[None, :] // CHUNK)))
    tri = tri.astype(np.float32)
    hr = np.arange(RWKV_CH)
    head_ones = ((hr[:, None] // RWKV_HEAD) == (hr[None, :] // RWKV_HEAD)).astype(np.float32)
    return masks, tri, head_ones


def _each(f, *cols):
    return [f(*xs) for xs in zip(*cols)]


def _unit_lower_inverse_minus_identity(labs, m16, m32, m64, passes):
    mm = functools.partial(_dot, passes=passes)
    d = [lab * m16 for lab in labs]
    d2 = _each(mm, d, d)
    d4 = _each(mm, d2, d2)
    dd2 = _each(mm, d, d2)
    d8 = _each(mm, d4, d4)
    n = _each(lambda x, x2, xx2: x2 - x - xx2, d, d2, dd2)
    n = _each(lambda x, y, xy: x + y + xy, n, d4, _each(mm, n, d4))
    n = _each(lambda x, y, xy: x + y + xy, n, d8, _each(mm, n, d8))
    for mask in (m32, m64):
        off = [lab * mask for lab in labs]
        w = _each(lambda o, no: o + no, off, _each(mm, n, off))
        w = _each(lambda x, xn: x + xn, w, _each(mm, w, n))
        n = _each(lambda x, y: x - y, n, w)
    return n


def _rwkv_kernel(h_ref, win_ref, mu_ref, w0_ref, w2_ref, a0_ref, a2_ref, g2_ref, kk_ref, ka_ref,
                 rk_ref, lg_ref, lb_ref, masks_ref, tri_ref, ones_ref, o_ref, s_ref, carry_ref):
    c = CHUNK
    rows = h_ref.shape[0]
    n_groups = RWKV_CH // GROUP_W
    w = GROUP_W

    @pl.when(pl.program_id(1) == 0)
    def _():
        s_ref[...] = jnp.zeros_like(s_ref)
        carry_ref[...] = jnp.zeros_like(carry_ref)

    z = jnp.dot(h_ref[...].astype(BF16), win_ref[...], preferred_element_type=F32)
    prev = pltpu.roll(z, 1, 0)
    row_id = lax.broadcasted_iota(jnp.int32, z.shape, 0)
    prev = jnp.where(row_id == 0, jnp.broadcast_to(carry_ref[7:8, :], z.shape), prev)
    carry_ref[...] = z[rows - 8:rows, :]
    z = z + (prev - z) * mu_ref[...]

    r = z[:, 0:RWKV_CH]
    k = z[:, RWKV_CH:2 * RWKV_CH]
    v = z[:, 2 * RWKV_CH:3 * RWKV_CH]
    wa_lo = z[:, 3 * RWKV_CH:3 * RWKV_CH + 128]
    g_lo = z[:, 3 * RWKV_CH + 128:3 * RWKV_CH + 256]

    wpre = w0_ref[...] + _dot(jnp.tanh(wa_lo), w2_ref[...], passes=3)
    ew = _sigmoid(wpre) * math.exp(-0.5)
    a = _sigmoid(a0_ref[...] + _dot(wa_lo, a2_ref[...]))
    gate = _dot(_sigmoid(g_lo), g2_ref[...])

    ones = ones_ref[...]
    kk = k * kk_ref[...]
    ss = _dot_exact_rhs(kk * kk, ones, parts=1)
    kk = kk * lax.rsqrt(jnp.maximum(ss, 1e-24))
    k2 = k * (1.0 + (a - 1.0) * ka_ref[...])
    kka = kk * a

    e0 = ew.astype(BF16)
    e1f = ew - e0.astype(F32)
    e1 = e1f.astype(BF16)
    e2 = (e1f - e1.astype(F32)).astype(BF16)
    tri = tri_ref[...]
    cum = (jnp.dot(tri, e0, preferred_element_type=F32) + jnp.dot(tri, e1, preferred_element_type=F32)
           + jnp.dot(tri, e2, preferred_element_type=F32))
    gam = jnp.exp(-cum)
    igam = jnp.exp(cum)
    r_t = r * gam
    ka_t = kk * jnp.exp(ew - cum)
    b_t = kka * igam
    k_t = k2 * igam

    strict, incl, m16, m32, m64, stack = (masks_ref[i] for i in range(6))
    p1 = 1
    mm = functools.partial(_dot, passes=p1)

    units = [(q, g) for q in range(rows // c) for g in range(n_groups)]

    def stacked(x):
        return [jnp.concatenate([x[q * c:(q + 1) * c, g * w:(g + 1) * w]] * HEADS_PER_GROUP, axis=0)
                * stack for q, g in units]

    xa, xr, xv, yb, yk = stacked(ka_t), stacked(r_t), stacked(v), stacked(b_t), stacked(k_t)
    xar = _each(lambda top, bot: jnp.concatenate([top, bot], axis=0), xa, xr)
    ybk = _each(lambda top, bot: jnp.concatenate([top, bot], axis=0), yb, yk)
    pbk = _each(lambda x, y: mm(x, y, NT), xar, ybk)
    lab = [p[:w, :w] * strict for p in pbk]
    arb = [p[w:, :w] * incl for p in pbk]
    lak = [p[:w, w:] * strict for p in pbk]
    ark = [p[w:, w:] * incl for p in pbk]
    n_inv = _unit_lower_inverse_minus_identity(lab, m16, m32, m64, p1)
    lv = _each(mm, lak, xv)
    rhs = _each(lambda left, right: jnp.concatenate([left, right], axis=1), xa, lv)
    gz = _each(lambda x, nx: x + nx, rhs, _each(mm, n_inv, rhs))
    corr = _each(mm, arb, gz)
    qh = _each(lambda x, cr: x - cr[:, :w], xr, corr)
    yloc = _each(lambda av, cr: av - cr[:, w:], _each(mm, ark, xv), corr)
    gzb = _each(lambda x, y: mm(x, y, TN), gz, yb)
    vk = _each(lambda x, y: mm(x, y, TN), xv, yk)

    state = [s_ref[g] for g in range(n_groups)]
    y_rows = []
    for q in range(rows // c):
        gam_end = gam[(q + 1) * c - 1:(q + 1) * c, :]
        ys = []
        for g in range(n_groups):
            i = q * n_groups + g
            s0 = state[g]
            y_st = mm(qh[i], s0, NT) + yloc[i]
            y_heads = y_st[0:c]
            for h in range(1, HEADS_PER_GROUP):
                y_heads = y_heads + y_st[h * c:(h + 1) * c]
            ys.append(y_heads)
            s_new = s0 - mm(s0, gzb[i][:w]) + vk[i] - gzb[i][w:]
            state[g] = s_new * gam_end[:, g * w:(g + 1) * w]
        y_rows.append(jnp.concatenate(ys, axis=1))
    for g in range(n_groups):
        s_ref[g] = state[g]
    y = jnp.concatenate(y_rows, axis=0)

    inv_n = 1.0 / RWKV_HEAD
    mean = _dot_exact_rhs(y, ones) * inv_n
    yc = y - mean
    var = _dot_exact_rhs(yc * yc, ones, parts=1) * inv_n
    yn = yc * lax.rsqrt(var + LNX_EPS) * lg_ref[...] + lb_ref[...]
    bonus = _dot_exact_rhs(r * k2 * rk_ref[...], ones, parts=1) * v
    o_ref[...] = ((yn + bonus) * gate).astype(o_ref.dtype)


def _rwkv_time_mix(h, w_in_rwkv, n_batch, shift_mu, w0, w2, a0, a2, g2, k_k, k_a, r_k, lnx_g, lnx_b):
    t, d = h.shape
    step_rows = CHUNK * RWKV_CHUNKS_PER_STEP
    n_steps = t // n_batch // step_rows
    masks, tri, head_ones = _rwkv_masks(step_rows)
    row = lambda v: v.reshape(1, -1).astype(F32)
    zeros64 = jnp.zeros((64, RWKV_CH), F32)
    w2p = jnp.concatenate([w2, zeros64], axis=0)
    a2p = jnp.concatenate([zeros64, a2], axis=0)
    full2 = lambda shape: pl.BlockSpec(shape, lambda b, c: (0, 0))
    vec = full2((1, RWKV_CH))
    return pl.pallas_call(
        _rwkv_kernel,
        grid=(n_batch, n_steps),
        in_specs=[pl.BlockSpec((step_rows, d), lambda b, c: (b * n_steps + c, 0)),
                  full2((d, RWKV_COLS)), full2((1, RWKV_COLS)), vec, full2((128, RWKV_CH)), vec, full2((128, RWKV_CH)),
                  full2((128, RWKV_CH)), vec, vec, vec, vec, vec,
                  pl.BlockSpec(masks.shape, lambda b, c: (0, 0, 0)),
                  full2((step_rows, step_rows)), full2((RWKV_CH, RWKV_CH))],
        out_specs=pl.BlockSpec((step_rows, RWKV_CH), lambda b, c: (b * n_steps + c, 0)),
        out_shape=jax.ShapeDtypeStruct((t, RWKV_CH), BF16),
        scratch_shapes=[pltpu.VMEM((RWKV_CH // GROUP_W, GROUP_W, GROUP_W), F32),
                        pltpu.VMEM((8, RWKV_COLS), F32)],
        compiler_params=_params("arbitrary", "arbitrary"),
        name="rwkv7_time_mix",
    )(h, w_in_rwkv, row(shift_mu), row(w0), w2p, row(a0), a2p, g2, row(k_k), row(k_a), row(r_k),
      row(lnx_g), row(lnx_b), jnp.asarray(masks), jnp.asarray(tri, BF16),
      jnp.asarray(head_ones, BF16))


SWIGLU_ROWS = 256


def _swiglu_rows(x_ref, wg_ref, wu_ref, wd_ref, rows):
    xb = x_ref[rows, :].astype(BF16)
    gate = jnp.dot(xb, wg_ref[...], preferred_element_type=F32)
    up = jnp.dot(xb, wu_ref[...], preferred_element_type=F32)
    act = (gate * _sigmoid(gate) * up).astype(BF16)
    return jnp.dot(act, wd_ref[...], preferred_element_type=F32)


def _row_blocks(n_rows):
    assert n_rows % SWIGLU_ROWS == 0
    return [slice(r0, r0 + SWIGLU_ROWS) for r0 in range(0, n_rows, SWIGLU_ROWS)]


def _mix_ffn_kernel(h_ref, a_ref, b_ref, wa_ref, wb_ref, g1_ref, b1_ref, wg_ref, wu_ref, wd_ref,
                    g2_ref, b2_ref, o_ref, h1_ref):
    for rows in _row_blocks(h_ref.shape[0]):
        mix = (jnp.dot(a_ref[rows, :], wa_ref[...], preferred_element_type=F32)
               + jnp.dot(b_ref[rows, :], wb_ref[...], preferred_element_type=F32))
        h1 = _layer_norm(ALPHA * h_ref[rows, :] + mix, g1_ref[...], b1_ref[...], LN_EPS)
        h1_ref[rows, :] = h1
        acc = ALPHA * h1 + _swiglu_rows(h1_ref, wg_ref, wu_ref, wd_ref, rows)
        o_ref[rows, :] = _layer_norm(acc, g2_ref[...], b2_ref[...], LN_EPS)


def _mix_ffn(h, a_out, b_out, w_a, w_b, g1, b1, wg, wu, wd, g2, b2, tm):
    t, d = h.shape
    dff = wg.shape[1]
    row = lambda v: v.reshape(1, -1)
    full = lambda shape: pl.BlockSpec(shape, lambda i: (0, 0))
    tile = lambda width: pl.BlockSpec((tm, width), lambda i: (i, 0))
    once = lambda shape: pl.BlockSpec(shape, lambda i: (0, 0), pipeline_mode=pl.Buffered(1))
    return pl.pallas_call(
        _mix_ffn_kernel,
        grid=(t // tm,),
        in_specs=[tile(d), tile(a_out.shape[1]), tile(b_out.shape[1]), full(w_a.shape), full(w_b.shape),
                  full((1, d)), full((1, d)), once((d, dff)), once((d, dff)), once((dff, d)),
                  full((1, d)), full((1, d))],
        out_specs=tile(d),
        out_shape=jax.ShapeDtypeStruct((t, d), F32),
        scratch_shapes=[pltpu.VMEM((tm, d), F32)],
        compiler_params=_params("parallel"),
        name="mix_ffn_ln",
    )(h, a_out, b_out, w_a, w_b, row(g1), row(b1), wg, wu, wd, row(g2), row(b2))


def _cast_kernel(*refs):
    n = len(refs) // 2
    for src, dst in zip(refs[:n], refs[n:]):
        dst[...] = src[...].astype(dst.dtype)


CAST_STEPS = 32


def _cast_expert_weights(ws):
    flat = [w.reshape(-1, w.shape[-1]) for w in ws]
    assert all(f.shape[0] % (8 * CAST_STEPS) == 0 for f in flat)
    specs = [pl.BlockSpec((f.shape[0] // CAST_STEPS, f.shape[1]), lambda i: (i, 0)) for f in flat]
    outs = pl.pallas_call(
        _cast_kernel,
        grid=(CAST_STEPS,),
        in_specs=specs,
        out_specs=specs,
        out_shape=[jax.ShapeDtypeStruct(f.shape, BF16) for f in flat],
        compiler_params=_params("parallel"),
        name="cast_bf16",
    )(*flat)
    return [o.reshape(w.shape) for o, w in zip(outs, ws)]


Q_COLS = N_Q_HEADS * HEAD_DIM
K_COLS = 2 * N_KV_HEADS * 128
V_COLS = N_KV_HEADS * 128
KV_COLS = K_COLS + V_COLS


def _qkv_rope_kernel(h_ref, w_ref, bias_ref, cos_ref, sin_ref, q_ref, kv_ref):
    acc = jnp.dot(h_ref[...].astype(BF16), w_ref[...], preferred_element_type=F32) + bias_ref[...]
    cos = cos_ref[...]
    sin = sin_ref[...]
    lane = lax.broadcasted_iota(jnp.int32, cos.shape, 1)
    first_half = (lane % HEAD_DIM) < (HEAD_DIM // 2)
    for c0 in range(0, Q_COLS + K_COLS, 128):
        x = acc[:, c0:c0 + 128]
        swapped = jnp.where(first_half, pltpu.roll(x, 128 - HEAD_DIM // 2, 1),
                            pltpu.roll(x, HEAD_DIM // 2, 1))
        y = x * cos + swapped * sin
        if c0 < Q_COLS:
            q_ref[:, c0:c0 + 128] = (y * (HEAD_DIM ** -0.5)).astype(BF16)
        else:
            kv_ref[:, c0 - Q_COLS:c0 - Q_COLS + 128] = y.astype(BF16)
    kv_ref[:, K_COLS:] = acc[:, Q_COLS + K_COLS:].astype(BF16)


def _rope_tables(lp):
    half = HEAD_DIM // 2
    inv = ROPE_THETA ** (-jnp.arange(half, dtype=F32) / half)
    pos = (jnp.arange(lp) - FRONT_PAD).astype(F32)
    ang = pos[:, None] * inv[None, :]
    cos = jnp.cos(ang)
    sin = jnp.sin(ang)
    cos = jnp.concatenate([cos, cos] * (128 // HEAD_DIM), axis=1)
    sin = jnp.concatenate([-sin, sin] * (128 // HEAD_DIM), axis=1)
    return cos, sin


def _qkv_rope(h, w, bias, lp, tm):
    t, d = h.shape
    n = w.shape[1]
    cos, sin = _rope_tables(lp)
    per_seq = lp // tm
    return pl.pallas_call(
        _qkv_rope_kernel,
        grid=(t // tm,),
        in_specs=[pl.BlockSpec((tm, d), lambda i: (i, 0)),
                  pl.BlockSpec((d, n), lambda i: (0, 0)),
                  pl.BlockSpec((1, n), lambda i: (0, 0)),
                  pl.BlockSpec((tm, 128), lambda i: (i % per_seq, 0)),
                  pl.BlockSpec((tm, 128), lambda i: (i % per_seq, 0))],
        out_specs=[pl.BlockSpec((tm, Q_COLS), lambda i: (i, 0)),
                   pl.BlockSpec((tm, KV_COLS), lambda i: (i, 0))],
        out_shape=[jax.ShapeDtypeStruct((t, Q_COLS), BF16), jax.ShapeDtypeStruct((t, KV_COLS), BF16)],
        compiler_params=_params("parallel"),
        name="qkv_rope",
    )(h, w, bias.reshape(1, -1), cos, sin)


def _qkv_weights(w_qkv, b_qkv):
    kv_w = N_KV_HEADS * HEAD_DIM
    wq, wk, wv = jnp.split(w_qkv, [Q_COLS, Q_COLS + kv_w], axis=1)
    bq, bk, bv = jnp.split(b_qkv[None], [Q_COLS, Q_COLS + kv_w], axis=1)

    def expand(m):
        zero = jnp.zeros((m.shape[0], HEAD_DIM), m.dtype)
        k_cols, v_cols = [], []
        for g in range(N_KV_HEADS):
            col = m[:, g * HEAD_DIM:(g + 1) * HEAD_DIM]
            k_cols += [col, zero, zero, col]
            v_cols += [col, col]
        return k_cols, v_cols

    w_cols = [wq] + expand(wk)[0] + expand(wv)[1]
    b_cols = [bq] + expand(bk)[0] + expand(bv)[1]
    return jnp.concatenate(w_cols, axis=1), jnp.concatenate(b_cols, axis=1)[0]


NEG = -1e30


def _attn_kernel(sink_ref, q_ref, kv_prev_ref, kv_cur_ref, o_ref):
    n = pl.program_id(1)
    blk = ATT_BLOCK
    kv = jnp.concatenate([kv_prev_ref[...], kv_cur_ref[...]], axis=0)
    qi = lax.broadcasted_iota(jnp.int32, (blk, blk), 0)
    kc = lax.broadcasted_iota(jnp.int32, (blk, blk), 1)
    use_prev = kc > qi
    first_cur = FRONT_PAD - n * blk
    valid = kc >= jnp.where(use_prev, first_cur + blk, first_cur)
    low = lax.broadcasted_iota(jnp.int32, (blk, 128), 1) < HEAD_DIM
    for pair in range(N_Q_HEADS // 2):
        grp = (2 * pair) // GROUP
        vd = kv[:, K_COLS + grp * 128:K_COLS + (grp + 1) * 128]
        qp = q_ref[:, pair * 128:(pair + 1) * 128]
        outs = []
        for half in range(2):
            head = 2 * pair + half
            kh = kv[:, (2 * grp + half) * 128:(2 * grp + half + 1) * 128]
            s = lax.dot_general(qp, kh, NT, preferred_element_type=F32)
            s = jnp.where(valid, jnp.where(use_prev, s[:, :blk], s[:, blk:]), NEG)
            sink = sink_ref[head]
            m = jnp.maximum(jnp.max(s, axis=-1, keepdims=True), sink)
            p = jnp.exp(s - m)
            denom = jnp.sum(p, axis=-1, keepdims=True) + jnp.exp(sink - m)
            p2 = jnp.concatenate([jnp.where(use_prev, p, 0.0), jnp.where(use_prev, 0.0, p)], axis=1)
            o = jnp.dot(p2.astype(BF16), vd, preferred_element_type=F32)
            outs.append(o * (1.0 / denom))
        o_ref[:, pair * 128:(pair + 1) * 128] = jnp.where(low, outs[0], outs[1]).astype(o_ref.dtype)


def _attention(q, kv, sinks, n_batch, lp):
    t = q.shape[0]
    nb = lp // ATT_BLOCK
    return pl.pallas_call(
        _attn_kernel,
        grid=(n_batch, nb),
        in_specs=[pl.BlockSpec(memory_space=pltpu.SMEM),
                  pl.BlockSpec((ATT_BLOCK, Q_COLS), lambda b, n: (b * nb + n, 0)),
                  pl.BlockSpec((ATT_BLOCK, KV_COLS), lambda b, n: (b * nb + jnp.maximum(n - 1, 0), 0)),
                  pl.BlockSpec((ATT_BLOCK, KV_COLS), lambda b, n: (b * nb + n, 0))],
        out_specs=pl.BlockSpec((ATT_BLOCK, Q_COLS), lambda b, n: (b * nb + n, 0)),
        out_shape=jax.ShapeDtypeStruct((t, Q_COLS), BF16),
        compiler_params=_params("parallel", "parallel"),
        name="swa_sink_attention",
    )(sinks.astype(F32), q, kv, kv)


ROW_E0, ROW_E1, ROW_W0, ROW_W1, ROW_R0, ROW_R1 = range(6)
ROUTE_FIELDS = 16
ROUTE_ROWS = 256


def _oproj_ln_route_kernel(lp, h_ref, x_ref, w_ref, bias_ref, g_ref, b_ref, rwt_ref, tri_ref,
                           o_ref, ob_ref, route_ref, count_ref, run_ref):
    @pl.when(pl.program_id(0) == 0)
    def _():
        run_ref[...] = jnp.zeros_like(run_ref)

    tm = h_ref.shape[0]
    for r0 in range(0, tm, ROUTE_ROWS):
        rows = slice(r0, r0 + ROUTE_ROWS)
        acc = ALPHA * h_ref[rows, :] + bias_ref[...] + jnp.dot(
            x_ref[rows, :], w_ref[...], preferred_element_type=F32)
        hn = _layer_norm(acc, g_ref[...], b_ref[...], LN_EPS)
        o_ref[rows, :] = hn
        ob_ref[rows, :] = hn.astype(BF16)

    sub = lax.broadcasted_iota(jnp.int32, (ROUTE_FIELDS, tm), 0)
    lane = lax.broadcasted_iota(jnp.int32, (ROUTE_FIELDS, tm), 1)
    logits = _dot(rwt_ref[...], o_ref[...], NT, passes=3)
    logits = jnp.where(sub < N_EXPERTS, logits, -jnp.inf)
    v1 = jnp.max(logits, axis=0, keepdims=True)
    i1 = jnp.min(jnp.where(logits == v1, sub, ROUTE_FIELDS), axis=0, keepdims=True)
    rest = jnp.where(sub == i1, -jnp.inf, logits)
    v2 = jnp.max(rest, axis=0, keepdims=True)
    i2 = jnp.min(jnp.where(rest == v2, sub, ROUTE_FIELDS), axis=0, keepdims=True)
    w1 = 1.0 / (1.0 + jnp.exp(v2 - v1))
    w2 = 1.0 - w1

    first = pl.program_id(0) * tm
    lead = ((first + tm - 1) // lp) * lp
    row = first + lane
    routed = (row < lead) | (row >= lead + FRONT_PAD + N_META)

    run = run_ref[:, 0:1]
    oh1 = jnp.where(routed & (sub == i1), 1.0, 0.0)
    oh2 = jnp.where(routed & (sub == i2), 1.0, 0.0)
    both = oh1 + oh2
    before = jnp.dot(both.astype(BF16), tri_ref[...], preferred_element_type=F32) + run
    r1 = jnp.sum(oh1 * before, axis=0, keepdims=True)
    r2 = jnp.sum(oh2 * before, axis=0, keepdims=True)
    run = run + jnp.sum(both, axis=1, keepdims=True)

    out = jnp.where(sub == ROW_E0, i1.astype(F32), 0.0)
    out = jnp.where(sub == ROW_E1, i2.astype(F32), out)
    out = jnp.where(sub == ROW_W0, w1, out)
    out = jnp.where(sub == ROW_W1, w2, out)
    out = jnp.where(sub == ROW_R0, r1, out)
    out = jnp.where(sub == ROW_R1, r2, out)
    route_ref[...] = out
    run_ref[...] = jnp.broadcast_to(run, run_ref.shape)
    count_ref[...] = jnp.broadcast_to(run, count_ref.shape)


def _oproj_ln_route(h, x, w, bias, g, b, router_w, lp, tm):
    t, d = h.shape
    assert tm % ROUTE_ROWS == 0 and lp > tm
    row = lambda v: v.reshape(1, -1)
    rwt = jnp.zeros((ROUTE_FIELDS, d), F32).at[:N_EXPERTS].set(router_w.T)
    tri = jnp.asarray(np.triu(np.ones((tm, tm), np.float32), 1), BF16)
    tile = lambda width: pl.BlockSpec((tm, width), lambda i: (i, 0))
    full = lambda shape: pl.BlockSpec(shape, lambda i: (0, 0))
    return pl.pallas_call(
        functools.partial(_oproj_ln_route_kernel, lp),
        grid=(t // tm,),
        in_specs=[tile(d), tile(x.shape[1]), full(w.shape), full((1, d)), full((1, d)), full((1, d)),
                  full((ROUTE_FIELDS, d)), full((tm, tm))],
        out_specs=[tile(d), tile(d), pl.BlockSpec((ROUTE_FIELDS, tm), lambda i: (0, i)),
                   full((ROUTE_FIELDS, 128))],
        out_shape=[jax.ShapeDtypeStruct((t, d), F32), jax.ShapeDtypeStruct((t, d), BF16),
                   jax.ShapeDtypeStruct((ROUTE_FIELDS, t), F32),
                   jax.ShapeDtypeStruct((ROUTE_FIELDS, 128), F32)],
        scratch_shapes=[pltpu.VMEM((ROUTE_FIELDS, 128), F32)],
        compiler_params=_params("arbitrary"),
        name="oproj_ln_route",
    )(h, x, w, row(bias), row(g), row(b), rwt, tri)


def _moe_kernel(n_ff, te_ref, tv_ref, x_ref, wg_ref, wu_ref, wd_ref, o_ref, acc_ref):
    i = pl.program_id(0)
    j = pl.program_id(1)
    last = n_ff - 1

    def step(first, final):
        for rows in _row_blocks(x_ref.shape[0]):
            part = _swiglu_rows(x_ref, wg_ref.at[0], wu_ref.at[0], wd_ref.at[0], rows)
            if not first:
                part = acc_ref[rows, :] + part
            if final:
                o_ref[rows, :] = part.astype(o_ref.dtype)
            else:
                acc_ref[rows, :] = part

    valid = tv_ref[i] > 0
    pl.when(valid & (j == 0))(functools.partial(step, True, False))
    if n_ff > 2:
        pl.when(valid & (j > 0) & (j < last))(functools.partial(step, False, False))
    pl.when(valid & (j == last))(functools.partial(step, False, True))

    @pl.when(jnp.logical_not(valid) & (j == last))
    def _():
        o_ref[...] = jnp.zeros_like(o_ref)


def _moe_experts(x_sorted, tile_expert, tile_valid, wg, wu, wd, tm, tf):
    n_rows, d = x_sorted.shape
    dff = wg.shape[2]
    ff_tile = lambda i, j, tv: j * tv[i] + (dff // tf - 1) * (1 - tv[i])
    grid_spec = pltpu.PrefetchScalarGridSpec(
        num_scalar_prefetch=2,
        grid=(n_rows // tm, dff // tf),
        in_specs=[pl.BlockSpec((tm, d), lambda i, j, te, tv: (i, 0)),
                  pl.BlockSpec((1, d, tf), lambda i, j, te, tv: (te[i], 0, ff_tile(i, j, tv))),
                  pl.BlockSpec((1, d, tf), lambda i, j, te, tv: (te[i], 0, ff_tile(i, j, tv))),
                  pl.BlockSpec((1, tf, d), lambda i, j, te, tv: (te[i], ff_tile(i, j, tv), 0))],
        out_specs=pl.BlockSpec((tm, d), lambda i, j, te, tv: (i, 0)),
        scratch_shapes=[pltpu.VMEM((tm, d), F32)],
    )
    assert dff // tf >= 2
    return pl.pallas_call(
        functools.partial(_moe_kernel, dff // tf),
        grid_spec=grid_spec,
        out_shape=jax.ShapeDtypeStruct((n_rows, d), BF16),
        compiler_params=_params("parallel", "arbitrary"),
        name="moe_expert_swiglu",
    )(tile_expert, tile_valid, x_sorted, wg, wu, wd)


def _route_plan(route, counts, tm, n_batch, lp):
    lead = FRONT_PAD + N_META
    seq_route = route.reshape(ROUTE_FIELDS, n_batch, lp)[:ROW_R1 + 1, :, lead:]
    seq_route = seq_route.reshape(ROW_R1 + 1, -1).astype(jnp.int32)
    top_idx = seq_route[ROW_E0:ROW_E1 + 1]
    rank = seq_route[ROW_R0:ROW_R1 + 1]
    n_assign = top_idx.size
    n_rows = n_assign + N_EXPERTS * tm
    counts = counts[:N_EXPERTS, 0].astype(jnp.int32)
    padded = ((counts + tm - 1) // tm) * tm
    ends = jnp.cumsum(padded)
    starts = ends - padded
    start_of = jnp.zeros_like(top_idx)
    for e in range(N_EXPERTS):
        start_of = jnp.where(top_idx == e, starts[e], start_of)
    pos = start_of + rank
    token = (jnp.arange(n_batch, dtype=jnp.int32)[:, None] * lp + lead
             + jnp.arange(lp - lead, dtype=jnp.int32)[None, :]).reshape(-1)
    src = jnp.arange(n_rows, dtype=jnp.int32) % (n_batch * lp)
    src = src.at[pos.reshape(-1)].set(jnp.tile(token, 2), unique_indices=True,
                                      mode="promise_in_bounds")
    tile_start = jnp.arange(n_rows // tm, dtype=jnp.int32) * tm
    tile_expert = jnp.sum((tile_start[:, None] >= ends[None, :]).astype(jnp.int32), axis=1)
    tile_expert = jnp.minimum(tile_expert, N_EXPERTS - 1)
    tile_valid = (tile_start < ends[-1]).astype(jnp.int32)
    return pos, src, tile_expert, tile_valid


def _combine_ln_kernel(h_ref, y1_ref, y2_ref, r_ref, sel_ref, g_ref, b_ref, o_ref):
    wts = None
    rem = r_ref[...]
    for _ in range(3):
        piece = rem.astype(BF16)
        term = lax.dot_general(piece, sel_ref[...], TN, preferred_element_type=F32)
        wts = term if wts is None else wts + term
        rem = rem - piece.astype(F32)
    reps = h_ref.shape[1] // 128
    w0 = jnp.concatenate([wts[:, :128]] * reps, axis=1)
    w1 = jnp.concatenate([wts[:, 128:]] * reps, axis=1)
    acc = ALPHA * h_ref[...] + w0 * y1_ref[...].astype(F32) + w1 * y2_ref[...].astype(F32)
    o_ref[...] = _layer_norm(acc, g_ref[...], b_ref[...], LN_EPS)


def _combine_ln(h, y1, y2, route, g, b, n_batch, lp, tm):
    d = h.shape[1]
    seq = lp - FRONT_PAD - N_META
    per_seq = seq // tm
    row = lambda v: v.reshape(1, -1)
    first_row = lambda b, j: pl.multiple_of(b * lp + FRONT_PAD + N_META + j * tm, ATT_BLOCK)
    h_tile = pl.BlockSpec((pl.Element(tm), pl.Element(d)), lambda b, j: (first_row(b, j), 0))
    route_tile = pl.BlockSpec((pl.Element(ROUTE_FIELDS), pl.Element(tm)),
                              lambda b, j: (0, first_row(b, j)))
    tile = pl.BlockSpec((tm, d), lambda b, j: (b * per_seq + j, 0))
    vec = pl.BlockSpec((1, d), lambda b, j: (0, 0))
    sel = np.zeros((ROUTE_FIELDS, 256), np.float32)
    sel[ROW_W0, :128] = 1.0
    sel[ROW_W1, 128:] = 1.0
    return pl.pallas_call(
        _combine_ln_kernel,
        grid=(n_batch, per_seq),
        in_specs=[h_tile, tile, tile, route_tile, pl.BlockSpec(sel.shape, lambda b, j: (0, 0)),
                  vec, vec],
        out_specs=tile,
        out_shape=jax.ShapeDtypeStruct((n_batch * seq, d), F32),
        compiler_params=_params("parallel", "parallel"),
        name="moe_combine_ln",
    )(h, y1, y2, route, jnp.asarray(sel, BF16), row(g), row(b))


def _pick_tile(n, target, mult=16):
    best = mult
    for cand in range(mult, min(n, target) + 1, mult):
        if n % cand == 0:
            best = cand
    return best


def kernel(x, meta_tokens, ev_w_in, ev_conv_w, ev_conv_b, ev_convnorm_g, ev_convnorm_b, ev_shift_mu, ev_w0, ev_w2, ev_a0, ev_a2, ev_g2, ev_k_k, ev_k_a, ev_r_k, ev_lnx_g, ev_lnx_b, ev_w_out, ev_ln1_g, ev_ln1_b, ev_ffn_gate, ev_ffn_up, ev_ffn_down, ev_ln2_g, ev_ln2_b, od_w_qkv, od_b_qkv, od_sinks, od_w_o, od_b_o, od_ln1_g, od_ln1_b, od_router, od_exp_gate, od_exp_up, od_exp_down, od_ln2_g, od_ln2_b):
    n_batch, seq, d = x.shape
    lp = FRONT_PAD + N_META + seq
    assert d == D_MODEL and lp % ATT_BLOCK == 0 and lp % (CHUNK * RWKV_CHUNKS_PER_STEP) == 0
    t = n_batch * lp
    tm = _pick_tile(t, 1024)
    tm_seq = _pick_tile(lp, 1056)

    meta = jnp.broadcast_to(meta_tokens[None].astype(x.dtype), (n_batch, N_META, d))
    h = jnp.concatenate([jnp.zeros((n_batch, FRONT_PAD, d), x.dtype), meta, x], axis=1)
    h = h.reshape(t, d)

    w_in = ev_w_in[0].astype(BF16)
    a_out = _conv_module(h, w_in[:, :2 * CONV_CH], ev_conv_w[0], ev_conv_b[0], ev_convnorm_g[0],
                         ev_convnorm_b[0])
    b_out = _rwkv_time_mix(h, w_in[:, 2 * CONV_CH:], n_batch, ev_shift_mu[0], ev_w0[0], ev_w2[0],
                           ev_a0[0], ev_a2[0], ev_g2[0], ev_k_k[0], ev_k_a[0], ev_r_k[0],
                           ev_lnx_g[0], ev_lnx_b[0])
    w_out = ev_w_out[0].astype(BF16)
    h = _mix_ffn(h, a_out, b_out, w_out[:CONV_CH], w_out[CONV_CH:], ev_ln1_g[0], ev_ln1_b[0],
                 ev_ffn_gate[0].astype(BF16), ev_ffn_up[0].astype(BF16), ev_ffn_down[0].astype(BF16),
                 ev_ln2_g[0], ev_ln2_b[0], _pick_tile(t, 1024, SWIGLU_ROWS))

    w_qkv, b_qkv = _qkv_weights(od_w_qkv[0], od_b_qkv[0])
    q, kv = _qkv_rope(h, w_qkv.astype(BF16), b_qkv, lp, tm_seq)
    att = _attention(q, kv, od_sinks[0], n_batch, lp)
    h, h_bf16, route, counts = _oproj_ln_route(h, att, od_w_o[0].astype(BF16), od_b_o[0], od_ln1_g[0],
                                               od_ln1_b[0], od_router[0], lp,
                                               _pick_tile(t, min(1024, lp - 1), ROUTE_ROWS))
    tm_moe = _pick_tile(2 * n_batch * seq, 1024, SWIGLU_ROWS)
    pos, src, tile_expert, tile_valid = _route_plan(route, counts, tm_moe, n_batch, lp)
    x_sorted = h_bf16.at[src].get(mode="promise_in_bounds")
    exp_w = _cast_expert_weights([od_exp_gate[0], od_exp_up[0], od_exp_down[0]])
    y_sorted = _moe_experts(x_sorted, tile_expert, tile_valid, *exp_w, tm_moe, 1792)
    y1 = y_sorted.at[pos[0]].get(mode="promise_in_bounds", unique_indices=True)
    y2 = y_sorted.at[pos[1]].get(mode="promise_in_bounds", unique_indices=True)
    out = _combine_ln(h, y1, y2, route, od_ln2_g[0], od_ln2_b[0], n_batch, lp, _pick_tile(seq, 1024))
    return out.reshape(n_batch, seq, d)
```
